```python
import jax, jax.numpy as jnp
from jax import lax
import numpy as np

D_MODEL = 1024
BATCH = 2
SEQ = 8192
DEPTH = 1

N_META = 16
HEAD_DIM = 64
FOX_HEADS = 8
RWKV_HEADS = 8
FOX_WIDTH = FOX_HEADS * HEAD_DIM
RWKV_WIDTH = RWKV_HEADS * HEAD_DIM
MIX_WIDTH = FOX_WIDTH + RWKV_WIDTH
Q_BLOCK = 128
DECAY_LORA = 64
AAA_LORA = 64
GATE_LORA = 128
FOX_COLS = 3 * FOX_WIDTH + FOX_HEADS
RWKV_COLS = 3 * RWKV_WIDTH + DECAY_LORA + AAA_LORA + GATE_LORA
IN_COLS = FOX_COLS + RWKV_COLS
N_EXPERTS = 32
TOP_K = 4
D_EXPERT = D_MODEL
SWIGLU_LIMIT = 7.0
SWIGLU_ALPHA = 1.702
MOE_BLOCK = 128
DEEPNORM_ALPHA = float((2 * DEPTH) ** 0.25)
DEEPNORM_BETA = float((8 * DEPTH) ** -0.25)
LN_EPS = 1e-5
GN_EPS = 64e-5
RMS_EPS = 1e-6
NEG_BIG = -1e30

kernel_name = "fox_rwkv7_hymba_moe_deepnorm"


def layer_norm(h, g, b):
    hf = h.astype(jnp.float32)
    mu = hf.mean(-1, keepdims=True)
    var = jnp.square(hf - mu).mean(-1, keepdims=True)
    return ((hf - mu) * lax.rsqrt(var + LN_EPS) * g + b).astype(h.dtype)


def fox_attention(q, k, v, logf):
    B, L, H, Dh = q.shape
    pad = Q_BLOCK - N_META
    pw = ((0, 0), (pad, 0), (0, 0), (0, 0))
    q, k, v = jnp.pad(q, pw), jnp.pad(k, pw), jnp.pad(v, pw)
    c = jnp.cumsum(jnp.pad(logf.astype(jnp.float32), ((0, 0), (pad, 0), (0, 0))), axis=1)
    Lp = q.shape[1]
    nb = Lp // Q_BLOCK
    cT = c.transpose(0, 2, 1)
    kpos = jnp.arange(Lp)
    qb = q.reshape(B, nb, Q_BLOCK, H, Dh).transpose(1, 0, 2, 3, 4)
    cb = c.reshape(B, nb, Q_BLOCK, H).transpose(1, 0, 3, 2)
    scale = HEAD_DIM ** -0.5

    def one_block(args):
        q_i, c_i, i = args
        s = jnp.einsum('bqhd,bkhd->bhqk', q_i, k).astype(jnp.float32) * scale
        s = s + c_i[..., None] - cT[:, :, None, :]
        qpos = i * Q_BLOCK + jnp.arange(Q_BLOCK)
        mask = (kpos[None, :] <= qpos[:, None]) & (kpos[None, :] >= pad)
        p = jax.nn.softmax(jnp.where(mask, s, NEG_BIG), axis=-1)
        return jnp.einsum('bhqk,bkhd->bqhd', p.astype(v.dtype), v)

    o = lax.map(one_block, (qb, cb, jnp.arange(nb)))
    return o.transpose(1, 0, 2, 3, 4).reshape(B, Lp, H, Dh)[:, pad:]


def rwkv7_scan(r, w, k, v, kk, a):
    B, L, H, N = r.shape
    xs = tuple(jnp.moveaxis(t.astype(jnp.float32), 1, 0) for t in (r, w, k, v, kk, a))

    def step(S, inp):
        r_t, w_t, k_t, v_t, kk_t, a_t = inp
        sa = jnp.einsum('bhij,bhj->bhi', S, -kk_t)
        S = S * w_t[:, :, None, :] + sa[..., None] * (kk_t * a_t)[:, :, None, :] \
            + v_t[..., None] * k_t[:, :, None, :]
        return S, jnp.einsum('bhij,bhj->bhi', S, r_t)

    S0 = jnp.zeros((B, H, N, N), jnp.float32)
    _, y = lax.scan(step, S0, xs)
    return jnp.moveaxis(y, 0, 1)


def hybrid_mixer(h, w_in, b_fgate, fox_norm_g, rwkv_mu, w0, w2, a0, a2, g2,
                 k_k, k_a, r_k, lnx_g, lnx_b, w_out):
    B, L, _ = h.shape
    z = h @ w_in
    z_fox, z_rwkv = z[..., :FOX_COLS], z[..., FOX_COLS:]

    q, kf, vf, f_logit = jnp.split(z_fox, [FOX_WIDTH, 2 * FOX_WIDTH, 3 * FOX_WIDTH], axis=-1)
    to_heads = lambda t: t.reshape(B, L, -1, HEAD_DIM)
    logf = jax.nn.log_sigmoid((f_logit + b_fgate).astype(jnp.float32))
    o_fox = fox_attention(to_heads(q), to_heads(kf), to_heads(vf), logf).astype(jnp.float32)
    o_fox = o_fox * lax.rsqrt(jnp.mean(jnp.square(o_fox), -1, keepdims=True) + RMS_EPS)
    o_fox = (o_fox.reshape(B, L, FOX_WIDTH) * fox_norm_g).astype(h.dtype)

    z_prev = jnp.pad(z_rwkv, ((0, 0), (1, 0), (0, 0)))[:, :-1]
    zs = z_rwkv + (z_prev - z_rwkv) * rwkv_mu
    r, kr, vr, dw, da, dg = jnp.split(
        zs, [RWKV_WIDTH, 2 * RWKV_WIDTH, 3 * RWKV_WIDTH, 3 * RWKV_WIDTH + DECAY_LORA,
             3 * RWKV_WIDTH + DECAY_LORA + AAA_LORA], axis=-1)
    w_log = -jax.nn.softplus(-(w0 + jnp.tanh(dw) @ w2)) - 0.5
    decay = jnp.exp(-jnp.exp(w_log.astype(jnp.float32)))
    a = jax.nn.sigmoid(a0 + da @ a2)
    g = jax.nn.sigmoid(dg) @ g2
    kk = to_heads(kr * k_k).astype(jnp.float32)
    kk = kk / jnp.maximum(jnp.linalg.norm(kk, axis=-1, keepdims=True), 1e-12)
    kr = kr * (1.0 + (a - 1.0) * k_a)
    rh, kh, vh, ah = to_heads(r), to_heads(kr), to_heads(vr), to_heads(a)
    y = rwkv7_scan(rh, to_heads(decay), kh, vh, kk, ah)
    mu = y.mean(-1, keepdims=True)
    var = jnp.square(y - mu).mean(-1, keepdims=True)
    y = ((y - mu) * lax.rsqrt(var + GN_EPS)).reshape(B, L, RWKV_WIDTH) * lnx_g + lnx_b
    bonus = jnp.sum(rh.astype(jnp.float32) * kh.astype(jnp.float32) * r_k, -1, keepdims=True) \
        * vh.astype(jnp.float32)
    y = ((y + bonus.reshape(B, L, RWKV_WIDTH)) * g).astype(h.dtype)

    return jnp.concatenate([o_fox, y], axis=-1) @ w_out


def moe_ffn(h, w_router, b_router, w_gu, b_gu, w_down, b_down):
    n, D = h.shape
    logits = (h @ w_router + b_router).astype(jnp.float32)
    top_vals, top_idx = lax.top_k(logits, TOP_K)
    gates = jax.nn.softmax(top_vals, axis=-1)
    flat_e = top_idx.reshape(-1)
    flat_tok = jnp.arange(n * TOP_K, dtype=jnp.int32) // TOP_K
    flat_g = gates.reshape(-1)
    order = jnp.argsort(flat_e, stable=True)
    e_sorted = flat_e[order]
    counts = jnp.bincount(flat_e, length=N_EXPERTS)
    start = jnp.cumsum(counts) - counts
    pcounts = (counts + MOE_BLOCK - 1) // MOE_BLOCK * MOE_BLOCK
    pend = jnp.cumsum(pcounts)
    pstart = pend - pcounts
    rank = jnp.arange(n * TOP_K) - start[e_sorted]
    dest = pstart[e_sorted] + rank
    n_blocks = -(-(n * TOP_K) // MOE_BLOCK) + N_EXPERTS
    cap = n_blocks * MOE_BLOCK
    slot_tok = jnp.full((cap,), n, jnp.int32).at[dest].set(flat_tok[order])
    slot_gate = jnp.zeros((cap,), jnp.float32).at[dest].set(flat_g[order])
    block_expert = jnp.minimum(
        jnp.searchsorted(pend, jnp.arange(n_blocks) * MOE_BLOCK, side='right'), N_EXPERTS - 1)
    h_pad = jnp.concatenate([h, jnp.zeros((1, D), h.dtype)], axis=0)
    xb = h_pad[slot_tok].reshape(n_blocks, MOE_BLOCK, D)

    def expert_block(args):
        xblk, e = args
        gu = xblk @ w_gu[e] + b_gu[e]
        gate, up = gu[:, :D_EXPERT], gu[:, D_EXPERT:]
        gate = jnp.minimum(gate, SWIGLU_LIMIT)
        up = jnp.clip(up, -SWIGLU_LIMIT, SWIGLU_LIMIT)
        act = (up + 1.0) * (gate * jax.nn.sigmoid(gate * SWIGLU_ALPHA))
        return act @ w_down[e] + b_down[e]

    yb = lax.map(expert_block, (xb, block_expert)).reshape(cap, D)
    y = jax.ops.segment_sum(yb * slot_gate[:, None].astype(yb.dtype), slot_tok, num_segments=n + 1)
    return y[:n].astype(h.dtype)


def setup_inputs(seed: int = 0) -> dict:
    key = jax.random.key(seed)
    ks = iter(jax.random.split(key, 40))
    nrm = lambda shape, s: jax.random.normal(next(ks), shape, jnp.float32) * s
    uni = lambda shape, lo, hi: jax.random.uniform(next(ks), shape, jnp.float32, lo, hi)
    Ld = DEPTH
    return {
        "x": nrm((BATCH, SEQ, D_MODEL), 1.0),
        "meta": nrm((N_META, D_MODEL), 1.0),
        "ln0_g": 1.0 + nrm((D_MODEL,), 0.02),
        "ln0_b": nrm((D_MODEL,), 0.01),
        "w_in": nrm((Ld, D_MODEL, IN_COLS), D_MODEL ** -0.5),
        "b_fgate": uni((Ld, FOX_HEADS), 1.0, 6.0),
        "fox_norm_g": 1.0 + nrm((Ld, FOX_WIDTH), 0.02),
        "rwkv_mu": uni((Ld, RWKV_COLS), 0.0, 1.0),
        "w0": uni((Ld, RWKV_WIDTH), -6.0, 0.0),
        "w2": nrm((Ld, DECAY_LORA, RWKV_WIDTH), 0.1 * DECAY_LORA ** -0.5),
        "a0": nrm((Ld, RWKV_WIDTH), 0.1),
        "a2": nrm((Ld, AAA_LORA, RWKV_WIDTH), 0.1 * AAA_LORA ** -0.5),
        "g2": nrm((Ld, GATE_LORA, RWKV_WIDTH), GATE_LORA ** -0.5),
        "k_k": 0.85 + nrm((Ld, RWKV_WIDTH), 0.02),
        "k_a": 1.0 + nrm((Ld, RWKV_WIDTH), 0.02),
        "r_k": nrm((Ld, RWKV_HEADS, HEAD_DIM), 0.1),
        "lnx_g": 1.0 + nrm((Ld, RWKV_WIDTH), 0.02),
        "lnx_b": nrm((Ld, RWKV_WIDTH), 0.01),
        "w_out": nrm((Ld, MIX_WIDTH, D_MODEL), MIX_WIDTH ** -0.5 * DEEPNORM_BETA),
        "ln1_g": 1.0 + nrm((Ld, D_MODEL), 0.02),
        "ln1_b": nrm((Ld, D_MODEL), 0.01),
        "w_router": nrm((Ld, D_MODEL, N_EXPERTS), D_MODEL ** -0.5),
        "b_router": nrm((Ld, N_EXPERTS), 0.01),
        "w_gu": nrm((Ld, N_EXPERTS, D_MODEL, 2 * D_EXPERT), D_MODEL ** -0.5),
        "b_gu": nrm((Ld, N_EXPERTS, 2 * D_EXPERT), 0.01),
        "w_down": nrm((Ld, N_EXPERTS, D_EXPERT, D_MODEL), D_EXPERT ** -0.5 * DEEPNORM_BETA),
        "b_down": nrm((Ld, N_EXPERTS, D_MODEL), 0.01),
        "ln2_g": 1.0 + nrm((Ld, D_MODEL), 0.02),
        "ln2_b": nrm((Ld, D_MODEL), 0.01),
    }


def reference(x, meta, ln0_g, ln0_b, w_in, b_fgate, fox_norm_g, rwkv_mu, w0, w2, a0, a2, g2,
              k_k, k_a, r_k, lnx_g, lnx_b, w_out, ln1_g, ln1_b, w_router, b_router,
              w_gu, b_gu, w_down, b_down, ln2_g, ln2_b):
    B = x.shape[0]
    h = jnp.concatenate([jnp.broadcast_to(meta[None].astype(x.dtype), (B, N_META, D_MODEL)), x], axis=1)
    h = layer_norm(h, ln0_g, ln0_b)
    for l in range(DEPTH):
        mix = hybrid_mixer(h, w_in[l], b_fgate[l], fox_norm_g[l], rwkv_mu[l], w0[l], w2[l],
                           a0[l], a2[l], g2[l], k_k[l], k_a[l], r_k[l], lnx_g[l], lnx_b[l], w_out[l])
        h = layer_norm(DEEPNORM_ALPHA * h + mix, ln1_g[l], ln1_b[l])
        Bh, L, D = h.shape
        ff = moe_ffn(h.reshape(Bh * L, D), w_router[l], b_router[l], w_gu[l], b_gu[l],
                     w_down[l], b_down[l]).reshape(Bh, L, D)
        h = layer_norm(DEEPNORM_ALPHA * h + ff, ln2_g[l], ln2_b[l])
    return h[:, N_META:]
```

```python
import functools

import jax
import jax.numpy as jnp
from jax import lax
from jax.experimental import pallas as pl
from jax.experimental.pallas import tpu as pltpu

N_META = 16
HEAD_DIM = 64
FOX_HEADS = 8
RWKV_HEADS = 8
FOX_WIDTH = FOX_HEADS * HEAD_DIM
RWKV_WIDTH = RWKV_HEADS * HEAD_DIM
DECAY_LORA = 64
AAA_LORA = 64
GATE_LORA = 128
TOP_K = 4
SWIGLU_LIMIT = 7.0
SWIGLU_ALPHA = 1.702
LN_EPS = 1e-5
GN_EPS = 64e-5
RMS_EPS = 1e-6
NEG_BIG = -1e30

LANES = 128
TILE = 512
CHUNK = 64
MOE_BLOCK = 256
DISPATCH_TOKENS = 256
COMBINE_TOKENS = 256
VMEM_LIMIT = 56 * 1024 * 1024

F32 = jnp.float32
BF16 = jnp.bfloat16


def _dot(a, b):
    return jnp.dot(a, b, preferred_element_type=F32)


def _dot_nt(a, b):
    return lax.dot_general(a, b, (((1,), (1,)), ((), ())), preferred_element_type=F32)


def _dot_tn(a, b):
    return lax.dot_general(a, b, (((0,), (0,)), ((), ())), preferred_element_type=F32)


def _layer_norm(x, g, b):
    mu = jnp.mean(x, axis=-1, keepdims=True)
    xc = x - mu
    var = jnp.mean(xc * xc, axis=-1, keepdims=True)
    return xc * lax.rsqrt(var + LN_EPS) * g + b


def _softplus(x):
    return jnp.maximum(x, 0.0) + jnp.log1p(jnp.exp(-jnp.abs(x)))


def _const_spec(shape):
    nd = len(shape)
    return pl.BlockSpec(shape, lambda *_: (0,) * nd)


def _params(sem):
    return pltpu.CompilerParams(dimension_semantics=sem, vmem_limit_bytes=VMEM_LIMIT)


def _inproj_body(x_ref, meta_ref, g_ref, b_ref, wqkv_ref, wf_ref, bf_ref, wr_ref,
                 q_ref, k_ref, v_ref, lf_ref, zr_ref):
    i = pl.program_id(1)
    x = jnp.where(i == 0, meta_ref[...], x_ref[0])
    h = _layer_norm(x, g_ref[...], b_ref[...])
    row = lax.broadcasted_iota(jnp.int32, (TILE, 1), 0)
    valid = jnp.logical_or(i > 0, row >= TILE - N_META)
    hb = jnp.where(valid, h, 0.0).astype(BF16)
    qkv = _dot(hb, wqkv_ref[...])
    w = FOX_WIDTH
    q_ref[0] = (qkv[:, :w] * (HEAD_DIM ** -0.5)).astype(BF16)
    k_ref[0] = qkv[:, w:2 * w].astype(BF16)
    v_ref[0] = qkv[:, 2 * w:].astype(BF16)
    f = _dot(hb, wf_ref[...]) + bf_ref[...]
    lf_ref[0] = jnp.where(valid, -_softplus(-f), 0.0)
    zr_ref[0] = _dot(hb, wr_ref[...]).astype(BF16)


def _inproj(x, meta_tile, g0, b0, wqkv, wf, bf, wr):
    bsz, seq, d = x.shape
    nt = seq // TILE + 1
    lp = nt * TILE
    rc = wr.shape[1]
    out_shape = (
        jax.ShapeDtypeStruct((bsz, lp, FOX_WIDTH), BF16),
        jax.ShapeDtypeStruct((bsz, lp, FOX_WIDTH), BF16),
        jax.ShapeDtypeStruct((bsz, lp, FOX_WIDTH), BF16),
        jax.ShapeDtypeStruct((bsz, lp, LANES), F32),
        jax.ShapeDtypeStruct((bsz, lp, rc), BF16),
    )
    tok = lambda c: pl.BlockSpec((1, TILE, c), lambda b, i: (b, i, 0))
    return pl.pallas_call(
        _inproj_body,
        out_shape=out_shape,
        grid=(bsz, nt),
        in_specs=[
            pl.BlockSpec((1, TILE, d), lambda b, i: (b, jnp.maximum(i - 1, 0), 0)),
            _const_spec((TILE, d)),
            _const_spec((1, d)), _const_spec((1, d)),
            _const_spec(wqkv.shape), _const_spec(wf.shape), _const_spec(bf.shape), _const_spec(wr.shape),
        ],
        out_specs=(tok(FOX_WIDTH), tok(FOX_WIDTH), tok(FOX_WIDTH), tok(LANES), tok(rc)),
        compiler_params=_params(("parallel", "parallel")),
        name="ln0_inproj",
    )(x, meta_tile, g0, b0, wqkv, wf, bf, wr)


def _cumsum_body(lf_ref, c_ref, ct_ref, *, nblk):
    r = lax.broadcasted_iota(jnp.int32, (LANES, LANES), 0)
    c = lax.broadcasted_iota(jnp.int32, (LANES, LANES), 1)
    tri = (c <= r).astype(F32)
    carry = jnp.zeros((1, LANES), F32)
    for j in range(nblk):
        sl = slice(j * LANES, (j + 1) * LANES)
        cs = jnp.dot(tri, lf_ref[0, sl, :], precision=lax.Precision.HIGHEST,
                     preferred_element_type=F32) + carry
        c_ref[0, sl, :] = cs
        ct_ref[0, :, sl] = cs.T[0:FOX_HEADS, :]
        carry = cs[LANES - 1:LANES, :]


def _cumsum(lf):
    bsz, lp, _ = lf.shape
    return pl.pallas_call(
        functools.partial(_cumsum_body, nblk=lp // LANES),
        out_shape=(jax.ShapeDtypeStruct((bsz, lp, LANES), F32),
                   jax.ShapeDtypeStruct((bsz, FOX_HEADS, lp), F32)),
        grid=(bsz,),
        in_specs=[pl.BlockSpec((1, lp, LANES), lambda b: (b, 0, 0))],
        out_specs=(pl.BlockSpec((1, lp, LANES), lambda b: (b, 0, 0)),
                   pl.BlockSpec((1, FOX_HEADS, lp), lambda b: (b, 0, 0))),
        compiler_params=_params(("parallel",)),
        name="forget_cumsum",
    )(lf)


def _fox_body(q_ref, k_ref, v_ref, c_ref, ct_ref, g_ref, o_ref, m_sc, l_sc, acc_sc):
    p = pl.program_id(1)
    qi = pl.program_id(2)
    ki = pl.program_id(3)

    @pl.when(ki == 0)
    def _init():
        m_sc[...] = jnp.full(m_sc.shape, NEG_BIG, F32)
        l_sc[...] = jnp.zeros(l_sc.shape, F32)
        acc_sc[...] = jnp.zeros(acc_sc.shape, F32)

    @pl.when(ki <= qi)
    def _step():
        q = q_ref[0]
        k = k_ref[0]
        v = v_ref[0]
        lane = lax.broadcasted_iota(jnp.int32, (TILE, LANES), 1)
        qpos = qi * TILE + lax.broadcasted_iota(jnp.int32, (TILE, 1), 0)
        kpos = ki * TILE + lax.broadcasted_iota(jnp.int32, (1, TILE), 1)
        mask = jnp.logical_and(kpos <= qpos, kpos >= TILE - N_META)
        c_tile = c_ref[0]
        ct = ct_ref[0]
        sub = lax.broadcasted_iota(jnp.int32, (FOX_HEADS, TILE), 0)
        for hh in range(2):
            head = 2 * p + hh
            in_head = jnp.logical_and(lane >= HEAD_DIM * hh, lane < HEAD_DIM * (hh + 1))
            qm = jnp.where(in_head, q, jnp.zeros_like(q))
            s = _dot_nt(qm, k)
            cq = jnp.sum(jnp.where(lane == head, c_tile, 0.0), axis=1, keepdims=True)
            ck = jnp.sum(jnp.where(sub == head, ct, 0.0), axis=0, keepdims=True)
            s = jnp.where(mask, s + cq - ck, NEG_BIG)
            m_old = m_sc[hh]
            m_new = jnp.maximum(m_old, jnp.max(s, axis=1, keepdims=True))
            alpha = jnp.exp(m_old - m_new)
            pm = jnp.exp(s - m_new)
            l_sc[hh] = alpha * l_sc[hh] + jnp.sum(pm, axis=1, keepdims=True)
            acc_sc[hh] = alpha * acc_sc[hh] + _dot(pm.astype(BF16), v)
            m_sc[hh] = m_new

    @pl.when(ki == qi)
    def _finish():
        lane = lax.broadcasted_iota(jnp.int32, (TILE, LANES), 1)
        lo = lane < HEAD_DIM
        o = jnp.where(lo, acc_sc[0] / l_sc[0], acc_sc[1] / l_sc[1])
        sq = o * o
        ss_lo = jnp.sum(jnp.where(lo, sq, 0.0), axis=1, keepdims=True)
        ss_hi = jnp.sum(jnp.where(lo, 0.0, sq), axis=1, keepdims=True)
        ms = jnp.where(lo, ss_lo, ss_hi) * (1.0 / HEAD_DIM)
        o_ref[0] = (o * lax.rsqrt(ms + RMS_EPS) * g_ref[...]).astype(BF16)


def _fox_attention(q, k, v, c, ct, g):
    bsz, lp, _ = q.shape
    nt = lp // TILE
    npair = FOX_WIDTH // LANES
    kv_spec = pl.BlockSpec((1, TILE, LANES), lambda b, p, qi, ki: (b, jnp.minimum(ki, qi), p))
    return pl.pallas_call(
        _fox_body,
        out_shape=jax.ShapeDtypeStruct((bsz, lp, FOX_WIDTH), BF16),
        grid=(bsz, npair, nt, nt),
        in_specs=[
            pl.BlockSpec((1, TILE, LANES), lambda b, p, qi, ki: (b, qi, p)),
            kv_spec, kv_spec,
            pl.BlockSpec((1, TILE, LANES), lambda b, p, qi, ki: (b, qi, 0)),
            pl.BlockSpec((1, FOX_HEADS, TILE), lambda b, p, qi, ki: (b, 0, jnp.minimum(ki, qi))),
            pl.BlockSpec((1, LANES), lambda b, p, qi, ki: (0, p)),
        ],
        out_specs=pl.BlockSpec((1, TILE, LANES), lambda b, p, qi, ki: (b, qi, p)),
        scratch_shapes=[pltpu.VMEM((2, TILE, 1), F32), pltpu.VMEM((2, TILE, 1), F32),
                        pltpu.VMEM((2, TILE, LANES), F32)],
        compiler_params=_params(("parallel", "parallel", "parallel", "arbitrary")),
        name="fox_attention",
    )(q, k, v, c, ct, g)


def _rwkv_body(z_ref, zp_ref, mu_ref, w0_ref, w2_ref, a0_ref, a2_ref, g2_ref, kk_ref, ka_ref, rk_ref,
               lg_ref, lb_ref, ones_ref, y_ref,
               ar_s, bt_s, kt_s, v_s, bkb_s, gam_s, s_s, yh_s, bon_s, gate_s):
    i = pl.program_id(1)
    nch = TILE // CHUNK
    nh = RWKV_HEADS
    w = RWKV_WIDTH

    @pl.when(i == 0)
    def _reset():
        s_s[...] = jnp.zeros(s_s.shape, F32)

    z = z_ref[0].astype(F32)
    prev = zp_ref[0][15:16, :].astype(F32)
    prev = jnp.where(i == 0, 0.0, prev)
    row = lax.broadcasted_iota(jnp.int32, (TILE, 1), 0)
    z_shift = jnp.where(row == 0, prev, pltpu.roll(z, 1, 0))
    zs = z + (z_shift - z) * mu_ref[...]
    r = zs[:, 0:w]
    kr = zs[:, w:2 * w]
    vr = zs[:, 2 * w:3 * w]
    dw = zs[:, 3 * w:3 * w + LANES]
    da = zs[:, 3 * w + LANES:3 * w + 2 * LANES]
    dg = zs[:, 3 * w + 2 * LANES:3 * w + 3 * LANES]
    w_log = -_softplus(-(w0_ref[...] + _dot(jnp.tanh(dw).astype(BF16), w2_ref[...]))) - 0.5
    lw = -jnp.exp(w_log)
    alpha = jax.nn.sigmoid(a0_ref[...] + _dot(da.astype(BF16), a2_ref[...]))
    gate_s[...] = _dot(jax.nn.sigmoid(dg).astype(BF16), g2_ref[...])
    ones = ones_ref[...]
    kk0 = kr * kk_ref[...]
    kk = kk0 * lax.rsqrt(jnp.maximum(_dot((kk0 * kk0).astype(BF16), ones), 1e-24))
    kmod = kr * (1.0 + (alpha - 1.0) * ka_ref[...])
    bon_s[...] = _dot((r * kmod * rk_ref[...]).astype(BF16), ones) * vr

    rc = row % CHUNK
    gcum = lw
    sh = 1
    while sh < CHUNK:
        gcum = gcum + jnp.where(rc >= sh, pltpu.roll(gcum, sh, 0), 0.0)
        sh *= 2
    glast = jnp.concatenate(
        [jnp.broadcast_to(gcum[c * CHUNK + CHUNK - 1:(c + 1) * CHUNK, :], (CHUNK, w)) for c in range(nch)], axis=0)
    e_in = jnp.exp(gcum)
    e_out = jnp.exp(-gcum)
    e_tail = jnp.exp(glast - gcum)
    a_t = (-kk) * jnp.exp(gcum - lw)
    r_t = r * e_in
    b_t = kk * alpha * e_out
    k_t = kmod * e_out
    b_bar = kk * alpha * e_tail
    k_bar = kmod * e_tail
    gam = jnp.exp(glast)
    for h in range(nh):
        sl = slice(h * HEAD_DIM, (h + 1) * HEAD_DIM)
        a_h = a_t[:, sl].astype(BF16)
        r_h = r_t[:, sl].astype(BF16)
        bb_h = b_bar[:, sl].astype(BF16)
        kb_h = k_bar[:, sl].astype(BF16)
        gam_h = gam[:, sl]
        bt_s[h] = b_t[:, sl].astype(BF16)
        kt_s[h] = k_t[:, sl].astype(BF16)
        v_s[h] = vr[:, sl].astype(BF16)
        for c in range(nch):
            rows = slice(c * CHUNK, (c + 1) * CHUNK)
            ar_s[h, c, 0:CHUNK, :] = a_h[rows]
            ar_s[h, c, CHUNK:2 * CHUNK, :] = r_h[rows]
            bkb_s[h, c, 0:CHUNK, :] = bb_h[rows]
            bkb_s[h, c, CHUNK:2 * CHUNK, :] = kb_h[rows]
            gam_s[h, c] = gam_h[c * CHUNK:c * CHUNK + 1]

    def chunk_pair(j, carry):
        r2 = lax.broadcasted_iota(jnp.int32, (2 * CHUNK, CHUNK), 0)
        c2 = lax.broadcasted_iota(jnp.int32, (2 * CHUNK, CHUNK), 1)
        tri2 = c2 < jnp.where(r2 < CHUNK, r2, r2 - CHUNK + 1)
        r1 = lax.broadcasted_iota(jnp.int32, (CHUNK, CHUNK), 0)
        c1 = lax.broadcasted_iota(jnp.int32, (CHUNK, CHUNK), 1)
        level_masks = []
        s = 1
        while s < CHUNK:
            level_masks.append(jnp.logical_and(((r1 ^ c1) & (-s)) == s, r1 > c1))
            s *= 2

        for cc in range(2):
            c = 2 * j + cc
            r0 = pl.multiple_of(c * CHUNK, CHUNK)
            for h in range(nh):
                ar = ar_s[h, c]
                vh = v_s[h, pl.ds(r0, CHUNK), :]
                pb = jnp.where(tri2, _dot_nt(ar, bt_s[h, pl.ds(r0, CHUNK), :]), 0.0)
                pk = jnp.where(tri2, _dot_nt(ar, kt_s[h, pl.ds(r0, CHUNK), :]), 0.0)
                n_ab = pb[0:CHUNK]
                a_rb = pb[CHUNK:2 * CHUNK]
                pv = _dot(pk.astype(BF16), vh)
                low = jnp.where(level_masks[0], n_ab, 0.0)
                for m in level_masks[1:]:
                    n_off = jnp.where(m, n_ab, 0.0)
                    x = n_off + _dot(n_off.astype(BF16), low.astype(BF16))
                    low = low + x + _dot(low.astype(BF16), x.astype(BF16))
                st = s_s[h]
                ars = _dot_nt(ar, st.astype(BF16))
                wm = ars[0:CHUNK] + pv[0:CHUNK]
                u = wm + _dot(low.astype(BF16), wm.astype(BF16))
                y = ars[CHUNK:2 * CHUNK] + pv[CHUNK:2 * CHUNK] + _dot(a_rb.astype(BF16), u.astype(BF16))
                uv = jnp.concatenate([u.astype(BF16), vh], axis=0)
                s_s[h] = st * gam_s[h, c] + _dot_tn(uv, bkb_s[h, c])
                yh_s[h, pl.ds(r0, CHUNK), :] = y
        return carry

    lax.fori_loop(0, nch // 2, chunk_pair, 0)

    for p in range(w // LANES):
        halves = []
        for hh in range(2):
            yh = yh_s[2 * p + hh]
            mu = jnp.mean(yh, axis=-1, keepdims=True)
            d = yh - mu
            var = jnp.mean(d * d, axis=-1, keepdims=True)
            halves.append(d * lax.rsqrt(var + GN_EPS))
        yn = jnp.concatenate(halves, axis=1)
        sl = slice(p * LANES, (p + 1) * LANES)
        out = (yn * lg_ref[:, sl] + lb_ref[:, sl] + bon_s[:, sl]) * gate_s[:, sl]
        y_ref[0, :, sl] = out.astype(BF16)


def _rwkv(zr, mu, w0, w2, a0, a2, g2, k_k, k_a, r_k, lnx_g, lnx_b, ones_bd):
    bsz, lp, rc = zr.shape
    nt = lp // TILE
    nch = TILE // CHUNK
    nh = RWKV_HEADS
    vec = _const_spec((1, RWKV_WIDTH))
    return pl.pallas_call(
        _rwkv_body,
        out_shape=jax.ShapeDtypeStruct((bsz, lp, RWKV_WIDTH), BF16),
        grid=(bsz, nt),
        in_specs=[
            pl.BlockSpec((1, TILE, rc), lambda b, i: (b, i, 0)),
            pl.BlockSpec((1, 16, rc), lambda b, i: (b, jnp.maximum(i * (TILE // 16) - 1, 0), 0)),
            _const_spec((1, rc)),
            vec, _const_spec(w2.shape), vec, _const_spec(a2.shape), _const_spec(g2.shape),
            vec, vec, vec, vec, vec,
            _const_spec(ones_bd.shape),
        ],
        out_specs=pl.BlockSpec((1, TILE, RWKV_WIDTH), lambda b, i: (b, i, 0)),
        scratch_shapes=[
            pltpu.VMEM((nh, nch, 2 * CHUNK, HEAD_DIM), BF16),
            pltpu.VMEM((nh, TILE, HEAD_DIM), BF16),
            pltpu.VMEM((nh, TILE, HEAD_DIM), BF16),
            pltpu.VMEM((nh, TILE, HEAD_DIM), BF16),
            pltpu.VMEM((nh, nch, 2 * CHUNK, HEAD_DIM), BF16),
            pltpu.VMEM((nh, nch, 1, HEAD_DIM), F32),
            pltpu.VMEM((nh, HEAD_DIM, HEAD_DIM), F32),
            pltpu.VMEM((nh, TILE, HEAD_DIM), F32),
            pltpu.VMEM((TILE, RWKV_WIDTH), F32),
            pltpu.VMEM((TILE, RWKV_WIDTH), F32),
        ],
        compiler_params=_params(("parallel", "arbitrary")),
        name="rwkv7_mix",
    )(zr, zr, mu, w0, w2, a0, a2, g2, k_k, k_a, r_k, lnx_g, lnx_b, ones_bd)


def _outproj_body(x_ref, of_ref, yr_ref, g0_ref, b0_ref, wo1_ref, wo2_ref, g1_ref, b1_ref, wr_ref, br_ref,
                  h1_ref, e_ref, gt_ref, rk_ref, cnt_ref, carry_s, *, alpha):
    first = jnp.logical_and(pl.program_id(0) == 0, pl.program_id(1) == 0)

    @pl.when(first)
    def _reset():
        carry_s[...] = jnp.zeros(carry_s.shape, F32)

    h0 = _layer_norm(x_ref[0], g0_ref[...], b0_ref[...])
    mix = _dot(of_ref[0], wo1_ref[...]) + _dot(yr_ref[0], wo2_ref[...])
    h1 = _layer_norm(alpha * h0 + mix, g1_ref[...], b1_ref[...])
    h1_ref[...] = h1
    logits = jnp.dot(h1, wr_ref[...], precision=lax.Precision.HIGHEST,
                     preferred_element_type=F32) + br_ref[...]
    lane = lax.broadcasted_iota(jnp.int32, (TILE, LANES), 1)
    lane_f = lane.astype(F32)
    cur = logits
    vals, idxs = [], []
    for _ in range(TOP_K):
        m = jnp.max(cur, axis=1, keepdims=True)
        idx = jnp.min(jnp.where(cur == m, lane_f, float(LANES)), axis=1, keepdims=True).astype(jnp.int32)
        vals.append(m)
        idxs.append(idx)
        cur = jnp.where(lane == idx, -jnp.inf, cur)
    exps = [jnp.exp(v - vals[0]) for v in vals]
    denom = exps[0] + exps[1] + exps[2] + exps[3]
    hot = [lane == idx for idx in idxs]
    multi = jnp.zeros((TILE, LANES), F32)
    for hk in hot:
        multi = multi + hk.astype(F32)
    rr = lax.broadcasted_iota(jnp.int32, (TILE, TILE), 0)
    cc = lax.broadcasted_iota(jnp.int32, (TILE, TILE), 1)
    before = _dot((cc < rr).astype(BF16), multi.astype(BF16)) + carry_s[...]
    e_out = jnp.zeros((TILE, LANES), jnp.int32)
    g_out = jnp.zeros((TILE, LANES), F32)
    r_out = jnp.zeros((TILE, LANES), F32)
    for kk in range(TOP_K):
        rank = jnp.sum(jnp.where(hot[kk], before, 0.0), axis=1, keepdims=True)
        e_out = jnp.where(lane == kk, idxs[kk], e_out)
        g_out = jnp.where(lane == kk, exps[kk] / denom, g_out)
        r_out = jnp.where(lane == kk, rank, r_out)
    e_ref[...] = e_out
    gt_ref[...] = g_out
    rk_ref[...] = r_out.astype(jnp.int32)
    carry_s[...] = carry_s[...] + jnp.sum(multi, axis=0, keepdims=True)
    cnt_ref[...] = carry_s[...]


def _outproj_router(x, o_fox, y_rwkv, g0, b0, wo1, wo2, g1, b1, w_router, b_router, alpha):
    bsz, seq, d = x.shape
    nt = seq // TILE
    n = bsz * seq
    row_blk = lambda c: pl.BlockSpec((TILE, c), lambda b, i: (b * nt + i, 0))
    mix_spec = pl.BlockSpec((1, TILE, FOX_WIDTH), lambda b, i: (b, i + 1, 0))
    return pl.pallas_call(
        functools.partial(_outproj_body, alpha=alpha),
        out_shape=(jax.ShapeDtypeStruct((n, d), F32),
                   jax.ShapeDtypeStruct((n, LANES), jnp.int32),
                   jax.ShapeDtypeStruct((n, LANES), F32),
                   jax.ShapeDtypeStruct((n, LANES), jnp.int32),
                   jax.ShapeDtypeStruct((1, LANES), F32)),
        grid=(bsz, nt),
        in_specs=[
            pl.BlockSpec((1, TILE, d), lambda b, i: (b, i, 0)),
            mix_spec, mix_spec,
            _const_spec((1, d)), _const_spec((1, d)),
            _const_spec(wo1.shape), _const_spec(wo2.shape),
            _const_spec((1, d)), _const_spec((1, d)),
            _const_spec(w_router.shape), _const_spec(b_router.shape),
        ],
        out_specs=(row_blk(d), row_blk(LANES), row_blk(LANES), row_blk(LANES), _const_spec((1, LANES))),
        scratch_shapes=[pltpu.VMEM((1, LANES), F32)],
        compiler_params=_params(("arbitrary", "arbitrary")),
        name="outproj_ln1_router",
    )(x, o_fox, y_rwkv, g0, b0, wo1, wo2, g1, b1, w_router, b_router)


def _dispatch_body(dest_ref, zstart_ref, nused_ref, h1_hbm, xb_hbm, zero_buf, sem, zsem, *, n_experts, n_blocks):
    i = pl.program_id(0)

    def zero_copy(start):
        start = pl.multiple_of(start, MOE_BLOCK)
        return pltpu.make_async_copy(zero_buf, xb_hbm.at[pl.ds(start, MOE_BLOCK)], zsem)

    @pl.when(i == 0)
    def _clear():
        zero_buf[...] = jnp.zeros(zero_buf.shape, F32)
        for e in range(n_experts):
            zero_copy(zstart_ref[e]).start()

        def start_tail(j, carry):
            zero_copy(j * MOE_BLOCK).start()
            return carry

        lax.fori_loop(nused_ref[0], n_blocks, start_tail, 0)
        for e in range(n_experts):
            zero_copy(zstart_ref[e]).wait()

        def wait_tail(j, carry):
            zero_copy(j * MOE_BLOCK).wait()
            return carry

        lax.fori_loop(nused_ref[0], n_blocks, wait_tail, 0)

    base = i * DISPATCH_TOKENS

    def row_copy(t, kk):
        return pltpu.make_async_copy(h1_hbm.at[pl.ds(base + t, 1)],
                                     xb_hbm.at[pl.ds(dest_ref[0, 0, t * TOP_K + kk], 1)], sem)

    def issue(t, carry):
        for kk in range(TOP_K):
            row_copy(t, kk).start()
        return carry

    lax.fori_loop(0, DISPATCH_TOKENS, issue, 0)

    def drain(t, carry):
        for kk in range(TOP_K):
            row_copy(t, kk).wait()
        return carry

    lax.fori_loop(0, DISPATCH_TOKENS, drain, 0)


def _dispatch(h1, dest, zstart, n_used, cap):
    n, d = h1.shape
    steps = n // DISPATCH_TOKENS
    dest3 = dest.reshape(steps, 1, DISPATCH_TOKENS * TOP_K)
    return pl.pallas_call(
        functools.partial(_dispatch_body, n_experts=zstart.shape[0], n_blocks=cap // MOE_BLOCK),
        out_shape=jax.ShapeDtypeStruct((cap, d), F32),
        grid=(steps,),
        in_specs=[
            pl.BlockSpec((1, 1, DISPATCH_TOKENS * TOP_K), lambda i: (i, 0, 0), memory_space=pltpu.SMEM),
            pl.BlockSpec(memory_space=pltpu.SMEM),
            pl.BlockSpec(memory_space=pltpu.SMEM),
            pl.BlockSpec(memory_space=pl.ANY),
        ],
        out_specs=pl.BlockSpec(memory_space=pl.ANY),
        scratch_shapes=[pltpu.VMEM((MOE_BLOCK, d), F32), pltpu.SemaphoreType.DMA(()), pltpu.SemaphoreType.DMA(())],
        compiler_params=_params(("arbitrary",)),
        name="moe_dispatch",
    )(dest3, zstart, n_used, h1)


def _moe_body(be_ref, nused_ref, xb_ref, wgu_ref, bgu_ref, wd_ref, bd_ref, yb_ref, wgu_bf, wd_bf):
    j = pl.program_id(0)

    @pl.when(j < nused_ref[0])
    def _run():
        prev = be_ref[jnp.maximum(j - 1, 0)]
        changed = jnp.logical_or(j == 0, be_ref[j] != prev)

        @pl.when(changed)
        def _cast():
            wgu_bf[...] = wgu_ref[0].astype(BF16)
            wd_bf[...] = wd_ref[0].astype(BF16)

        de = wd_bf.shape[0]
        gu = _dot(xb_ref[...].astype(BF16), wgu_bf[...]) + bgu_ref[0]
        gate = jnp.minimum(gu[:, :de], SWIGLU_LIMIT)
        up = jnp.clip(gu[:, de:], -SWIGLU_LIMIT, SWIGLU_LIMIT)
        act = (up + 1.0) * (gate * jax.nn.sigmoid(gate * SWIGLU_ALPHA))
        yb_ref[...] = _dot(act.astype(BF16), wd_bf[...]) + bd_ref[0]

    @pl.when(j >= nused_ref[0])
    def _unused():
        yb_ref[...] = jnp.zeros(yb_ref.shape, F32)


def _moe_experts(xb, block_expert, n_used, w_gu, b_gu, w_down, b_down):
    cap, d = xb.shape
    ne, _, d2 = w_gu.shape
    de = w_down.shape[1]
    nblk = cap // MOE_BLOCK
    blk = lambda j, be, nu: (jnp.minimum(j, nu[0] - 1), 0)
    wsel = lambda j, be, nu: (be[jnp.minimum(j, nu[0] - 1)], 0, 0)
    grid_spec = pltpu.PrefetchScalarGridSpec(
        num_scalar_prefetch=2,
        grid=(nblk,),
        in_specs=[
            pl.BlockSpec((MOE_BLOCK, d), blk),
            pl.BlockSpec((1, d, d2), wsel),
            pl.BlockSpec((1, 1, d2), wsel),
            pl.BlockSpec((1, de, d), wsel),
            pl.BlockSpec((1, 1, d), wsel),
        ],
        out_specs=pl.BlockSpec((MOE_BLOCK, d), lambda j, be, nu: (j, 0)),
        scratch_shapes=[pltpu.VMEM((d, d2), BF16), pltpu.VMEM((de, d), BF16)],
    )
    return pl.pallas_call(
        _moe_body,
        out_shape=jax.ShapeDtypeStruct((cap, d), F32),
        grid_spec=grid_spec,
        compiler_params=_params(("arbitrary",)),
        name="moe_experts",
    )(block_expert, n_used, xb, w_gu, b_gu.reshape(ne, 1, d2), w_down, b_down.reshape(ne, 1, d))


def _combine_body(dest_ref, gt_ref, h1_ref, yb_hbm, g2_ref, b2_ref, o_ref, ybuf, sem, *, alpha):
    def row_copy(t, kk):
        return pltpu.make_async_copy(yb_hbm.at[pl.ds(dest_ref[0, 0, t * TOP_K + kk], 1)],
                                     ybuf.at[kk, pl.ds(t, 1)], sem)

    def issue(t, carry):
        for kk in range(TOP_K):
            row_copy(t, kk).start()
        return carry

    lax.fori_loop(0, COMBINE_TOKENS, issue, 0)

    def drain(t, carry):
        for kk in range(TOP_K):
            row_copy(t, kk).wait()
        return carry

    lax.fori_loop(0, COMBINE_TOKENS, drain, 0)

    gates = gt_ref[...]
    lane = lax.broadcasted_iota(jnp.int32, gates.shape, 1)
    ff = jnp.zeros(h1_ref.shape, F32)
    for kk in range(TOP_K):
        gk = jnp.sum(jnp.where(lane == kk, gates, 0.0), axis=1, keepdims=True)
        ff = ff + gk * ybuf[kk]
    o_ref[0] = _layer_norm(alpha * h1_ref[...] + ff, g2_ref[...], b2_ref[...])


def _combine(h1, yb, dest, gates, g2, b2, bsz, seq, alpha):
    n, d = h1.shape
    steps = n // COMBINE_TOKENS
    per_b = seq // COMBINE_TOKENS
    dest3 = dest.reshape(steps, 1, COMBINE_TOKENS * TOP_K)
    return pl.pallas_call(
        functools.partial(_combine_body, alpha=alpha),
        out_shape=jax.ShapeDtypeStruct((bsz, seq, d), F32),
        grid=(steps,),
        in_specs=[
            pl.BlockSpec((1, 1, COMBINE_TOKENS * TOP_K), lambda i: (i, 0, 0), memory_space=pltpu.SMEM),
            pl.BlockSpec((COMBINE_TOKENS, LANES), lambda i: (i, 0)),
            pl.BlockSpec((COMBINE_TOKENS, d), lambda i: (i, 0)),
            pl.BlockSpec(memory_space=pl.ANY),
            _const_spec((1, d)), _const_spec((1, d)),
        ],
        out_specs=pl.BlockSpec((1, COMBINE_TOKENS, d), lambda i: (i // per_b, i % per_b, 0)),
        scratch_shapes=[pltpu.VMEM((TOP_K, COMBINE_TOKENS, d), F32), pltpu.SemaphoreType.DMA(())],
        compiler_params=_params(("arbitrary",)),
        name="moe_combine_ln2",
    )(dest3, gates, h1, yb, g2, b2)


def _pad_cols(a, width):
    return jnp.pad(a, ((0, 0), (0, width - a.shape[1])))


def kernel(x, meta, ln0_g, ln0_b, w_in, b_fgate, fox_norm_g, rwkv_mu, w0, w2, a0, a2, g2, k_k, k_a, r_k, lnx_g, lnx_b, w_out, ln1_g, ln1_b, w_router, b_router, w_gu, b_gu, w_down, b_down, ln2_g, ln2_b):
    bsz, seq, d = x.shape
    depth = w_in.shape[0]
    assert depth == 1, "meta-token rows are dropped before the MoE, which is only valid for one layer"
    assert seq % TILE == 0 and seq % COMBINE_TOKENS == 0 and seq % DISPATCH_TOKENS == 0
    alpha = float((2 * depth) ** 0.25)
    n = bsz * seq
    ne = w_gu.shape[1]
    row = lambda a: a.reshape(1, -1)

    w = w_in[0]
    f3 = 3 * FOX_WIDTH
    r3 = 3 * RWKV_WIDTH
    wqkv = w[:, :f3].astype(BF16)
    wf = _pad_cols(w[:, f3:f3 + FOX_HEADS], LANES).astype(BF16)
    bf = _pad_cols(row(b_fgate[0]), LANES)
    wr_raw = w[:, f3 + FOX_HEADS:]

    def regroup(a):
        return jnp.concatenate([
            a[:, :r3],
            _pad_cols(a[:, r3:r3 + DECAY_LORA], LANES),
            _pad_cols(a[:, r3 + DECAY_LORA:r3 + DECAY_LORA + AAA_LORA], LANES),
            a[:, r3 + DECAY_LORA + AAA_LORA:],
        ], axis=1)

    wr = regroup(wr_raw).astype(BF16)
    mu = regroup(row(rwkv_mu[0]))
    w2p = jnp.pad(w2[0], ((0, LANES - DECAY_LORA), (0, 0))).astype(BF16)
    a2p = jnp.pad(a2[0], ((0, LANES - AAA_LORA), (0, 0))).astype(BF16)
    g2b = g2[0].astype(BF16)
    hid = jnp.arange(RWKV_WIDTH) // HEAD_DIM
    ones_bd = (hid[:, None] == hid[None, :]).astype(BF16)
    meta_tile = jnp.pad(meta.astype(x.dtype), ((TILE - N_META, 0), (0, 0)))

    q, k, v, lf, zr = _inproj(x, meta_tile, row(ln0_g), row(ln0_b), wqkv, wf, bf, wr)
    c, ct = _cumsum(lf)
    o_fox = _fox_attention(q, k, v, c, ct, row(fox_norm_g[0]))
    y_rwkv = _rwkv(zr, mu, row(w0[0]), w2p, row(a0[0]), a2p, g2b, row(k_k[0]), row(k_a[0]),
                   row(r_k[0]), row(lnx_g[0]), row(lnx_b[0]), ones_bd)

    wo = w_out[0].astype(BF16)
    w_rt = _pad_cols(w_router[0], LANES)
    b_rt = jnp.concatenate([row(b_router[0]), jnp.full((1, LANES - ne), NEG_BIG, F32)], axis=1)
    h1, e_pad, gates, rank_pad, counts_f = _outproj_router(
        x, o_fox, y_rwkv, row(ln0_g), row(ln0_b), wo[:FOX_WIDTH], wo[FOX_WIDTH:],
        row(ln1_g[0]), row(ln1_b[0]), w_rt, b_rt, alpha)

    counts = counts_f[0, :ne].astype(jnp.int32)
    pcounts = (counts + MOE_BLOCK - 1) // MOE_BLOCK * MOE_BLOCK
    pend = jnp.cumsum(pcounts)
    pstart = pend - pcounts
    nblk = n * TOP_K // MOE_BLOCK + ne
    cap = nblk * MOE_BLOCK
    dest = (pstart[e_pad[:, :TOP_K]] + rank_pad[:, :TOP_K]).reshape(-1)
    block_row = jnp.arange(nblk, dtype=jnp.int32) * MOE_BLOCK
    block_expert = jnp.minimum(
        jnp.sum((pend[None, :] <= block_row[:, None]).astype(jnp.int32), axis=1), ne - 1)
    n_used = (pend[-1:] // MOE_BLOCK).astype(jnp.int32)
    zstart = jnp.maximum(pend - MOE_BLOCK, 0).astype(jnp.int32)

    xb = _dispatch(h1, dest, zstart, n_used, cap)
    yb = _moe_experts(xb, block_expert, n_used, w_gu[0], b_gu[0], w_down[0], b_down[0])
    return _combine(h1, yb, dest, gates, row(ln2_g[0]), row(ln2_b[0]), bsz, seq, alpha)
```

```python
import functools
import math

import jax
import jax.numpy as jnp
from jax import lax
from jax.experimental import pallas as pl
from jax.experimental.pallas import tpu as pltpu

N_META = 16
HEAD_DIM = 64
FOX_HEADS = 8
RWKV_HEADS = 8
FOX_WIDTH = FOX_HEADS * HEAD_DIM
RWKV_WIDTH = RWKV_HEADS * HEAD_DIM
DECAY_LORA = 64
AAA_LORA = 64
GATE_LORA = 128
TOP_K = 4
SWIGLU_LIMIT = 7.0
SWIGLU_ALPHA = 1.702
LN_EPS = 1e-5
GN_EPS = 64e-5
RMS_EPS = 1e-6
NEG_BIG = -1e30
LOG2E = math.log2(math.e)

LANES = 128
MXU_DIM = 256
TILE = 512
CHUNK = 64
FOX_STEP_HEADS = 4
FOX_ROW_BLOCK = 128
MOE_BLOCK = 256
DISPATCH_TOKENS = 256
COMBINE_TOKENS = 256
DMA_UNROLL = 8
VMEM_LIMIT = 56 * 1024 * 1024

Q_BIAS_LANE = HEAD_DIM
K_BIAS_LANE = HEAD_DIM + 3
V_ONES_LANE = HEAD_DIM

F32 = jnp.float32
BF16 = jnp.bfloat16


def _dot(a, b):
    return jnp.dot(a, b, preferred_element_type=F32)


def _dot_nt(a, b):
    return lax.dot_general(a, b, (((1,), (1,)), ((), ())), preferred_element_type=F32)


def _dot_tn(a, b):
    return lax.dot_general(a, b, (((0,), (0,)), ((), ())), preferred_element_type=F32)


def _layer_norm(x, g, b):
    mu = jnp.mean(x, axis=-1, keepdims=True)
    xc = x - mu
    var = jnp.mean(xc * xc, axis=-1, keepdims=True)
    return xc * lax.rsqrt(var + LN_EPS) * g + b


def _softplus(x):
    return jnp.maximum(x, 0.0) + jnp.log1p(jnp.exp(-jnp.abs(x)))


def _row_cumsum(x, period, row):
    pos = row % period
    sh = 1
    while sh < period:
        x = x + jnp.where(pos >= sh, pltpu.roll(x, sh, 0), 0.0)
        sh *= 2
    return x


def _const_spec(shape):
    nd = len(shape)
    return pl.BlockSpec(shape, lambda *_: (0,) * nd)


def _params(sem):
    return pltpu.CompilerParams(dimension_semantics=sem, vmem_limit_bytes=VMEM_LIMIT)


def _inproj_body(x_ref, meta_ref, g_ref, b_ref, wqkv_ref, wf_ref, bf_ref, wr_ref,
                 qa_ref, ka_ref, va_ref, zr_ref, carry_s):
    i = pl.program_id(1)

    @pl.when(i == 0)
    def _reset():
        carry_s[...] = jnp.zeros(carry_s.shape, F32)

    x = jnp.where(i == 0, meta_ref[...], x_ref[0])
    h = _layer_norm(x, g_ref[...], b_ref[...])
    row = lax.broadcasted_iota(jnp.int32, (TILE, 1), 0)
    valid = jnp.logical_or(i > 0, row >= TILE - N_META)
    hb = jnp.where(valid, h, 0.0).astype(BF16)
    zr_ref[0] = _dot(hb, wr_ref[...]).astype(BF16)

    f = _dot(hb, wf_ref[...]) + bf_ref[...]
    lf = jnp.where(valid, -_softplus(-f), 0.0)
    cs = _row_cumsum(lf, TILE, row) + carry_s[...]
    carry_s[...] = cs[TILE - 1:TILE, :]
    c2 = cs * LOG2E

    qkv = _dot(hb, wqkv_ref[...])
    w = FOX_WIDTH
    qs = qkv[:, :w] * (LOG2E * HEAD_DIM ** -0.5)
    ks = qkv[:, w:2 * w]
    vs = qkv[:, 2 * w:]
    lane = lax.broadcasted_iota(jnp.int32, (TILE, LANES), 1)
    in_head = lane < HEAD_DIM
    for hd in range(FOX_HEADS):
        pair = slice(LANES * (hd // 2), LANES * (hd // 2 + 1))

        def head_lanes(a):
            slab = a[:, pair]
            return slab if hd % 2 == 0 else pltpu.roll(slab, HEAD_DIM, 1)

        col = jnp.broadcast_to(c2[:, hd:hd + 1], (TILE, LANES))
        hi = col.astype(BF16).astype(F32)
        rem = col - hi
        mid = rem.astype(BF16).astype(F32)
        lo = rem - mid
        q_bias = jnp.where(lane == Q_BIAS_LANE, hi, jnp.where(lane == Q_BIAS_LANE + 1, mid, lo))
        q_tail = jnp.where(lane < Q_BIAS_LANE + 3, q_bias, jnp.where(lane < Q_BIAS_LANE + 6, 1.0, 0.0))
        k_bias = jnp.where(lane == K_BIAS_LANE, -hi, jnp.where(lane == K_BIAS_LANE + 1, -mid, -lo))
        k_bias = jnp.where(valid, k_bias, jnp.where(lane == K_BIAS_LANE, NEG_BIG, 0.0))
        k_tail = jnp.where(lane < K_BIAS_LANE, 1.0, jnp.where(lane < K_BIAS_LANE + 3, k_bias, 0.0))
        qa_ref[0, hd] = jnp.where(in_head, head_lanes(qs), q_tail).astype(BF16)
        ka_ref[0, hd] = jnp.where(in_head, head_lanes(ks), k_tail).astype(BF16)
        va_ref[0, hd] = jnp.where(in_head, head_lanes(vs), jnp.where(lane == V_ONES_LANE, 1.0, 0.0)).astype(BF16)


def _inproj(x, meta_tile, g0, b0, wqkv, wf, bf, wr):
    bsz, seq, d = x.shape
    nt = seq // TILE + 1
    lp = nt * TILE
    rc = wr.shape[1]
    slab = jax.ShapeDtypeStruct((bsz, FOX_HEADS, lp, LANES), BF16)
    slab_spec = pl.BlockSpec((1, FOX_HEADS, TILE, LANES), lambda b, i: (b, 0, i, 0))
    return pl.pallas_call(
        _inproj_body,
        out_shape=(slab, slab, slab, jax.ShapeDtypeStruct((bsz, lp, rc), BF16)),
        grid=(bsz, nt),
        in_specs=[
            pl.BlockSpec((1, TILE, d), lambda b, i: (b, jnp.maximum(i - 1, 0), 0)),
            _const_spec((TILE, d)),
            _const_spec((1, d)), _const_spec((1, d)),
            _const_spec(wqkv.shape), _const_spec(wf.shape), _const_spec(bf.shape), _const_spec(wr.shape),
        ],
        out_specs=(slab_spec, slab_spec, slab_spec, pl.BlockSpec((1, TILE, rc), lambda b, i: (b, i, 0))),
        scratch_shapes=[pltpu.VMEM((1, LANES), F32)],
        compiler_params=_params(("parallel", "arbitrary")),
        name="ln0_inproj",
    )(x, meta_tile, g0, b0, wqkv, wf, bf, wr)


def _fox_body(q_ref, k_ref, v_ref, g_ref, o_ref, m_sc, acc_sc):
    qi = pl.program_id(2)
    ki = pl.program_id(3)

    @pl.when(ki == 0)
    def _init():
        m_sc[...] = jnp.full(m_sc.shape, NEG_BIG, F32)
        acc_sc[...] = jnp.zeros(acc_sc.shape, F32)

    heads = range(FOX_STEP_HEADS)

    def scores(hh):
        return _dot_nt(q_ref[0, hh], k_ref[0, hh])

    def softmax_pv(hh, s, causal):
        for rb in range(TILE // FOX_ROW_BLOCK):
            rows = slice(rb * FOX_ROW_BLOCK, (rb + 1) * FOX_ROW_BLOCK)
            sb = s[rows]
            if causal:
                rr = lax.broadcasted_iota(jnp.int32, (FOX_ROW_BLOCK, TILE), 0) + rb * FOX_ROW_BLOCK
                cc = lax.broadcasted_iota(jnp.int32, (FOX_ROW_BLOCK, TILE), 1)
                sb = jnp.where(cc <= rr, sb, NEG_BIG)
            m_old = m_sc[hh, rows]
            m_col = jnp.maximum(m_old[:, 0:1], jnp.max(sb, axis=1, keepdims=True))
            m_new = jnp.broadcast_to(m_col, (FOX_ROW_BLOCK, LANES))
            p = jnp.exp2(sb - m_col).astype(BF16)
            acc_sc[hh, rows] = jnp.exp2(m_old - m_new) * acc_sc[hh, rows] + _dot(p, v_ref[0, hh])
            m_sc[hh, rows] = m_new

    def step(causal):
        s_next = scores(0)
        for hh in heads:
            s_cur = s_next
            if hh + 1 < FOX_STEP_HEADS:
                s_next = scores(hh + 1)
            softmax_pv(hh, s_cur, causal)

    @pl.when(ki < qi)
    def _off_diagonal():
        step(False)

    @pl.when(ki == qi)
    def _diagonal():
        step(True)
        lane = lax.broadcasted_iota(jnp.int32, (TILE, LANES), 1)
        lo = lane < HEAD_DIM
        normed = []
        for hh in heads:
            acc = acc_sc[hh]
            o = acc / acc[:, V_ONES_LANE:V_ONES_LANE + 1]
            ms = jnp.sum(jnp.where(lo, o * o, 0.0), axis=1, keepdims=True) * (1.0 / HEAD_DIM)
            normed.append(o * lax.rsqrt(ms + RMS_EPS))
        for pr in range(FOX_STEP_HEADS // 2):
            o = jnp.where(lo, normed[2 * pr], pltpu.roll(normed[2 * pr + 1], HEAD_DIM, 1))
            sl = slice(pr * LANES, (pr + 1) * LANES)
            o_ref[0, :, sl] = (o * g_ref[:, sl]).astype(BF16)


def _fox_attention(qa, ka, va, g):
    bsz, _, lp, _ = qa.shape
    nt = lp // TILE
    ngroup = FOX_HEADS // FOX_STEP_HEADS
    ow = FOX_STEP_HEADS * HEAD_DIM
    kv_spec = pl.BlockSpec((1, FOX_STEP_HEADS, TILE, LANES), lambda b, p, qi, ki: (b, p, jnp.minimum(ki, qi), 0))
    return pl.pallas_call(
        _fox_body,
        out_shape=jax.ShapeDtypeStruct((bsz, lp, FOX_WIDTH), BF16),
        grid=(bsz, ngroup, nt, nt),
        in_specs=[
            pl.BlockSpec((1, FOX_STEP_HEADS, TILE, LANES), lambda b, p, qi, ki: (b, p, qi, 0)),
            kv_spec, kv_spec,
            pl.BlockSpec((1, ow), lambda b, p, qi, ki: (0, p)),
        ],
        out_specs=pl.BlockSpec((1, TILE, ow), lambda b, p, qi, ki: (b, qi, p)),
        scratch_shapes=[pltpu.VMEM((FOX_STEP_HEADS, TILE, LANES), F32),
                        pltpu.VMEM((FOX_STEP_HEADS, TILE, LANES), F32)],
        compiler_params=_params(("parallel", "parallel", "parallel", "arbitrary")),
        name="fox_attention",
    )(qa, ka, va, g)


def _interleave(*gens):
    gens = list(gens)
    while gens:
        for gen in list(gens):
            try:
                next(gen)
            except StopIteration:
                gens.remove(gen)


def _rwkv_body(z_ref, zp_ref, mu_ref, w0_ref, w2_ref, a0_ref, a2_ref, g2_ref, kk_ref, ka_ref, rk_ref,
               lg_ref, lb_ref, ones_ref, y_ref,
               ar_s, bt_s, kt_s, v_s, bkb_s, gam_s, s_s, y_s, bon_s, gate_s, low_s, arb_s, pv_s):
    i = pl.program_id(1)
    nch = TILE // CHUNK
    w = RWKV_WIDTH
    gw = MXU_DIM
    ngrp = w // gw

    @pl.when(i == 0)
    def _reset():
        s_s[...] = jnp.zeros(s_s.shape, F32)

    z = z_ref[0].astype(F32)
    prev = zp_ref[0][15:16, :].astype(F32)
    prev = jnp.where(i == 0, 0.0, prev)
    row = lax.broadcasted_iota(jnp.int32, (TILE, 1), 0)
    z_shift = jnp.where(row == 0, prev, pltpu.roll(z, 1, 0))
    zs = z + (z_shift - z) * mu_ref[...]
    r = zs[:, 0:w]
    kr = zs[:, w:2 * w]
    vr = zs[:, 2 * w:3 * w]
    dw = zs[:, 3 * w:3 * w + LANES]
    da = zs[:, 3 * w + LANES:3 * w + 2 * LANES]
    dg = zs[:, 3 * w + 2 * LANES:3 * w + 3 * LANES]
    lw = -math.exp(-0.5) * jax.nn.sigmoid(w0_ref[...] + _dot(jnp.tanh(dw).astype(BF16), w2_ref[...]))
    alpha = jax.nn.sigmoid(a0_ref[...] + _dot(da.astype(BF16), a2_ref[...]))
    gate_s[...] = _dot(jax.nn.sigmoid(dg).astype(BF16), g2_ref[...])
    ones = ones_ref[...]
    kk0 = kr * kk_ref[...]
    kk = kk0 * lax.rsqrt(jnp.maximum(_dot((kk0 * kk0).astype(BF16), ones), 1e-24))
    kmod = kr * (1.0 + (alpha - 1.0) * ka_ref[...])
    bon_s[...] = _dot((r * kmod * rk_ref[...]).astype(BF16), ones) * vr

    gcum = _row_cumsum(lw, CHUNK, row)
    glast = jnp.concatenate(
        [jnp.broadcast_to(gcum[c * CHUNK + CHUNK - 1:(c + 1) * CHUNK, :], (CHUNK, w)) for c in range(nch)], axis=0)
    e_out = jnp.exp(-gcum)
    e_tail = jnp.exp(glast - gcum)
    a_t = ((-kk) * jnp.exp(gcum - lw)).astype(BF16)
    r_t = (r * jnp.exp(gcum)).astype(BF16)
    bt_s[...] = (kk * alpha * e_out).astype(BF16)
    kt_s[...] = (kmod * e_out).astype(BF16)
    v_s[...] = vr.astype(BF16)
    b_bar = (kk * alpha * e_tail).astype(BF16)
    k_bar = (kmod * e_tail).astype(BF16)
    gam = jnp.exp(glast)
    for c in range(nch):
        rows = slice(c * CHUNK, (c + 1) * CHUNK)
        ar_s[c, 0:CHUNK, :] = a_t[rows]
        ar_s[c, CHUNK:2 * CHUNK, :] = r_t[rows]
        bkb_s[c, 0:CHUNK, :] = b_bar[rows]
        bkb_s[c, CHUNK:2 * CHUNK, :] = k_bar[rows]
        gam_s[c] = gam[c * CHUNK:c * CHUNK + 1]

    rb = lax.broadcasted_iota(jnp.int32, (gw, gw), 0)
    cb = lax.broadcasted_iota(jnp.int32, (gw, gw), 1)
    same_head = (rb // HEAD_DIM) == (cb // HEAD_DIM)
    r2 = lax.broadcasted_iota(jnp.int32, (2 * CHUNK, gw), 0)
    c2 = lax.broadcasted_iota(jnp.int32, (2 * CHUNK, gw), 1) % CHUNK
    tri2 = c2 < jnp.where(r2 < CHUNK, r2, r2 - CHUNK + 1)
    r1 = lax.broadcasted_iota(jnp.int32, (CHUNK, gw), 0)
    c1 = lax.broadcasted_iota(jnp.int32, (CHUNK, gw), 1) % CHUNK
    level_masks = []
    s = 1
    while s < CHUNK:
        level_masks.append(jnp.logical_and(((r1 ^ c1) & (-s)) == s, r1 > c1))
        s *= 2

    def block_diag(x):
        xb = x.astype(BF16)
        return jnp.where(same_head, jnp.concatenate([xb] * (gw // CHUNK), axis=0), jnp.zeros((), BF16))

    def lanes(g):
        return slice(g * gw, (g + 1) * gw)

    def intra_chunk(chunks):
        keys = [(c, g) for c in chunks for g in range(ngrp)]
        n_ab, low = {}, {}
        for c, g in keys:
            rows = slice(c * CHUNK, (c + 1) * CHUNK)
            ar = ar_s[c, :, lanes(g)]
            pb = jnp.where(tri2, _dot_nt(ar, block_diag(bt_s[rows, lanes(g)])), 0.0)
            pk = jnp.where(tri2, _dot_nt(ar, block_diag(kt_s[rows, lanes(g)])), 0.0)
            n_ab[c, g] = pb[0:CHUNK]
            arb_s[c, :, lanes(g)] = pb[CHUNK:2 * CHUNK].astype(BF16)
            pv_s[c, :, lanes(g)] = _dot(pk.astype(BF16), block_diag(v_s[rows, lanes(g)]))
            low[c, g] = jnp.where(level_masks[0], n_ab[c, g], 0.0)
        yield
        for m in level_masks[1:]:
            x = {}
            for key in keys:
                n_off = jnp.where(m, n_ab[key], 0.0)
                x[key] = n_off + _dot(n_off.astype(BF16), block_diag(low[key]))
            yield
            for key in keys:
                low[key] = low[key] + x[key] + _dot(low[key].astype(BF16), block_diag(x[key]))
            yield
        for c, g in keys:
            low_s[c, :, lanes(g)] = low[c, g].astype(BF16)

    def inter_chunk(chunks):
        for c in chunks:
            rows = slice(c * CHUNK, (c + 1) * CHUNK)
            st, ars, u = {}, {}, {}
            for g in range(ngrp):
                st[g] = s_s[g]
                ars[g] = _dot_nt(ar_s[c, :, lanes(g)], st[g].astype(BF16))
            yield
            for g in range(ngrp):
                wm = ars[g][0:CHUNK] + pv_s[c, 0:CHUNK, lanes(g)]
                u[g] = wm + _dot(low_s[c, :, lanes(g)], block_diag(wm))
            yield
            for g in range(ngrp):
                y_s[rows, lanes(g)] = (ars[g][CHUNK:2 * CHUNK] + pv_s[c, CHUNK:2 * CHUNK, lanes(g)]
                                       + _dot(arb_s[c, :, lanes(g)], block_diag(u[g])))
                uv = jnp.concatenate([u[g].astype(BF16), v_s[rows, lanes(g)]], axis=0)
                upd = _dot_tn(uv, bkb_s[c, :, lanes(g)])
                s_s[g] = st[g] * gam_s[c][:, lanes(g)] + jnp.where(same_head, upd, 0.0)
            yield

    half = nch // 2
    _interleave(intra_chunk(range(0, half)))
    _interleave(intra_chunk(range(half, nch)), inter_chunk(range(0, half)))
    _interleave(inter_chunk(range(half, nch)))

    y = y_s[...]
    mean = _dot(y.astype(BF16), ones) * (1.0 / HEAD_DIM)
    d = y - mean
    var = _dot((d * d).astype(BF16), ones) * (1.0 / HEAD_DIM)
    yn = d * lax.rsqrt(var + GN_EPS)
    y_ref[0] = ((yn * lg_ref[...] + lb_ref[...] + bon_s[...]) * gate_s[...]).astype(BF16)


def _rwkv(zr, mu, w0, w2, a0, a2, g2, k_k, k_a, r_k, lnx_g, lnx_b, ones_bd):
    bsz, lp, rc = zr.shape
    nt = lp // TILE
    nch = TILE // CHUNK
    w = RWKV_WIDTH
    vec = _const_spec((1, w))
    return pl.pallas_call(
        _rwkv_body,
        out_shape=jax.ShapeDtypeStruct((bsz, lp, w), BF16),
        grid=(bsz, nt),
        in_specs=[
            pl.BlockSpec((1, TILE, rc), lambda b, i: (b, i, 0)),
            pl.BlockSpec((1, 16, rc), lambda b, i: (b, jnp.maximum(i * (TILE // 16) - 1, 0), 0)),
            _const_spec((1, rc)),
            vec, _const_spec(w2.shape), vec, _const_spec(a2.shape), _const_spec(g2.shape),
            vec, vec, vec, vec, vec,
            _const_spec(ones_bd.shape),
        ],
        out_specs=pl.BlockSpec((1, TILE, w), lambda b, i: (b, i, 0)),
        scratch_shapes=[
            pltpu.VMEM((nch, 2 * CHUNK, w), BF16),
            pltpu.VMEM((TILE, w), BF16),
            pltpu.VMEM((TILE, w), BF16),
            pltpu.VMEM((TILE, w), BF16),
            pltpu.VMEM((nch, 2 * CHUNK, w), BF16),
            pltpu.VMEM((nch, 1, w), F32),
            pltpu.VMEM((w // MXU_DIM, MXU_DIM, MXU_DIM), F32),
            pltpu.VMEM((TILE, w), F32),
            pltpu.VMEM((TILE, w), F32),
            pltpu.VMEM((TILE, w), F32),
            pltpu.VMEM((nch, CHUNK, w), BF16),
            pltpu.VMEM((nch, CHUNK, w), BF16),
            pltpu.VMEM((nch, 2 * CHUNK, w), F32),
        ],
        compiler_params=_params(("parallel", "arbitrary")),
        name="rwkv7_mix",
    )(zr, zr, mu, w0, w2, a0, a2, g2, k_k, k_a, r_k, lnx_g, lnx_b, ones_bd)


def _outproj_body(x_ref, of_ref, yr_ref, g0_ref, b0_ref, wo1_ref, wo2_ref, g1_ref, b1_ref, wr_ref, br_ref,
                  h1_ref, e_ref, gt_ref, rk_ref, cnt_ref, carry_s, *, alpha):
    first = jnp.logical_and(pl.program_id(0) == 0, pl.program_id(1) == 0)

    @pl.when(first)
    def _reset():
        carry_s[...] = jnp.zeros(carry_s.shape, F32)

    h0 = _layer_norm(x_ref[0], g0_ref[...], b0_ref[...])
    mix = _dot(of_ref[0], wo1_ref[...]) + _dot(yr_ref[0], wo2_ref[...])
    h1 = _layer_norm(alpha * h0 + mix, g1_ref[...], b1_ref[...])
    h1_ref[...] = h1
    logits = jnp.dot(h1, wr_ref[...], precision=lax.Precision.HIGHEST,
                     preferred_element_type=F32) + br_ref[...]
    lane = lax.broadcasted_iota(jnp.int32, (TILE, LANES), 1)
    lane_f = lane.astype(F32)
    cur = logits
    vals, idxs = [], []
    for _ in range(TOP_K):
        m = jnp.max(cur, axis=1, keepdims=True)
        idx = jnp.min(jnp.where(cur == m, lane_f, float(LANES)), axis=1, keepdims=True).astype(jnp.int32)
        vals.append(m)
        idxs.append(idx)
        cur = jnp.where(lane == idx, -jnp.inf, cur)
    exps = [jnp.exp(v - vals[0]) for v in vals]
    denom = exps[0] + exps[1] + exps[2] + exps[3]
    hot = [lane == idx for idx in idxs]
    multi = jnp.zeros((TILE, LANES), F32)
    for hk in hot:
        multi = multi + hk.astype(F32)
    rr = lax.broadcasted_iota(jnp.int32, (TILE, TILE), 0)
    cc = lax.broadcasted_iota(jnp.int32, (TILE, TILE), 1)
    before = _dot((cc < rr).astype(BF16), multi.astype(BF16)) + carry_s[...]
    e_out = jnp.zeros((TILE, LANES), jnp.int32)
    g_out = jnp.zeros((TILE, LANES), F32)
    r_out = jnp.zeros((TILE, LANES), F32)
    for kk in range(TOP_K):
        rank = jnp.sum(jnp.where(hot[kk], before, 0.0), axis=1, keepdims=True)
        e_out = jnp.where(lane == kk, idxs[kk], e_out)
        g_out = jnp.where(lane == kk, exps[kk] / denom, g_out)
        r_out = jnp.where(lane == kk, rank, r_out)
    e_ref[...] = e_out
    gt_ref[...] = g_out
    rk_ref[...] = r_out.astype(jnp.int32)
    carry_s[...] = carry_s[...] + jnp.sum(multi, axis=0, keepdims=True)
    cnt_ref[...] = carry_s[...]


def _outproj_router(x, o_fox, y_rwkv, g0, b0, wo1, wo2, g1, b1, w_router, b_router, alpha):
    bsz, seq, d = x.shape
    nt = seq // TILE
    n = bsz * seq
    row_blk = lambda c: pl.BlockSpec((TILE, c), lambda b, i: (b * nt + i, 0))
    mix_spec = pl.BlockSpec((1, TILE, FOX_WIDTH), lambda b, i: (b, i + 1, 0))
    return pl.pallas_call(
        functools.partial(_outproj_body, alpha=alpha),
        out_shape=(jax.ShapeDtypeStruct((n, d), F32),
                   jax.ShapeDtypeStruct((n, LANES), jnp.int32),
                   jax.ShapeDtypeStruct((n, LANES), F32),
                   jax.ShapeDtypeStruct((n, LANES), jnp.int32),
                   jax.ShapeDtypeStruct((1, LANES), F32)),
        grid=(bsz, nt),
        in_specs=[
            pl.BlockSpec((1, TILE, d), lambda b, i: (b, i, 0)),
            mix_spec, mix_spec,
            _const_spec((1, d)), _const_spec((1, d)),
            _const_spec(wo1.shape), _const_spec(wo2.shape),
            _const_spec((1, d)), _const_spec((1, d)),
            _const_spec(w_router.shape), _const_spec(b_router.shape),
        ],
        out_specs=(row_blk(d), row_blk(LANES), row_blk(LANES), row_blk(LANES), _const_spec((1, LANES))),
        scratch_shapes=[pltpu.VMEM((1, LANES), F32)],
        compiler_params=_params(("arbitrary", "arbitrary")),
        name="outproj_ln1_router",
    )(x, o_fox, y_rwkv, g0, b0, wo1, wo2, g1, b1, w_router, b_router)


def _dispatch_body(dest_ref, zstart_ref, nused_ref, h1_ref, xb_hbm, zero_buf, sem, zsem, *, n_experts, n_blocks):
    i = pl.program_id(0)

    def zero_copy(start):
        start = pl.multiple_of(start, MOE_BLOCK)
        return pltpu.make_async_copy(zero_buf, xb_hbm.at[pl.ds(start, MOE_BLOCK)], zsem)

    @pl.when(i == 0)
    def _clear():
        zero_buf[...] = jnp.zeros(zero_buf.shape, F32)
        for e in range(n_experts):
            zero_copy(zstart_ref[e]).start()

        def start_tail(j, carry):
            zero_copy(j * MOE_BLOCK).start()
            return carry

        lax.fori_loop(nused_ref[0], n_blocks, start_tail, 0)
        for e in range(n_experts):
            zero_copy(zstart_ref[e]).wait()

        def wait_tail(j, carry):
            zero_copy(j * MOE_BLOCK).wait()
            return carry

        lax.fori_loop(nused_ref[0], n_blocks, wait_tail, 0)

    def row_copy(t, kk):
        return pltpu.make_async_copy(h1_ref.at[pl.ds(t, 1)],
                                     xb_hbm.at[pl.ds(dest_ref[0, 0, t * TOP_K + kk], 1)], sem)

    def issue(t, carry):
        for kk in range(TOP_K):
            row_copy(t, kk).start()
        return carry

    lax.fori_loop(0, DISPATCH_TOKENS, issue, 0, unroll=DMA_UNROLL)

    def drain(t, carry):
        for kk in range(TOP_K):
            row_copy(t, kk).wait()
        return carry

    lax.fori_loop(0, DISPATCH_TOKENS, drain, 0, unroll=DMA_UNROLL)


def _dispatch(h1, dest, zstart, n_used, cap):
    n, d = h1.shape
    steps = n // DISPATCH_TOKENS
    dest3 = dest.reshape(steps, 1, DISPATCH_TOKENS * TOP_K)
    return pl.pallas_call(
        functools.partial(_dispatch_body, n_experts=zstart.shape[0], n_blocks=cap // MOE_BLOCK),
        out_shape=jax.ShapeDtypeStruct((cap, d), F32),
        grid=(steps,),
        in_specs=[
            pl.BlockSpec((1, 1, DISPATCH_TOKENS * TOP_K), lambda i: (i, 0, 0), memory_space=pltpu.SMEM),
            pl.BlockSpec(memory_space=pltpu.SMEM),
            pl.BlockSpec(memory_space=pltpu.SMEM),
            pl.BlockSpec((DISPATCH_TOKENS, d), lambda i: (i, 0)),
        ],
        out_specs=pl.BlockSpec(memory_space=pl.ANY),
        scratch_shapes=[pltpu.VMEM((MOE_BLOCK, d), F32), pltpu.SemaphoreType.DMA(()), pltpu.SemaphoreType.DMA(())],
        compiler_params=_params(("arbitrary",)),
        name="moe_dispatch",
    )(dest3, zstart, n_used, h1)


def _moe_body(be_ref, nused_ref, xb_ref, wgu_ref, bgu_ref, wd_ref, bd_ref, yb_ref, wgu_bf, wd_bf):
    j = pl.program_id(0)

    @pl.when(j < nused_ref[0])
    def _run():
        prev = be_ref[jnp.maximum(j - 1, 0)]
        changed = jnp.logical_or(j == 0, be_ref[j] != prev)

        @pl.when(changed)
        def _cast():
            wgu_bf[...] = wgu_ref[0].astype(BF16)
            wd_bf[...] = wd_ref[0].astype(BF16)

        de = wd_bf.shape[0]
        gu = _dot(xb_ref[...].astype(BF16), wgu_bf[...]) + bgu_ref[0]
        gate = jnp.minimum(gu[:, :de], SWIGLU_LIMIT)
        up = jnp.clip(gu[:, de:], -SWIGLU_LIMIT, SWIGLU_LIMIT)
        act = (up + 1.0) * (gate * jax.nn.sigmoid(gate * SWIGLU_ALPHA))
        yb_ref[...] = _dot(act.astype(BF16), wd_bf[...]) + bd_ref[0]

    @pl.when(j >= nused_ref[0])
    def _unused():
        yb_ref[...] = jnp.zeros(yb_ref.shape, F32)


def _moe_experts(xb, block_expert, n_used, w_gu, b_gu, w_down, b_down):
    cap, d = xb.shape
    ne, _, d2 = w_gu.shape
    de = w_down.shape[1]
    nblk = cap // MOE_BLOCK
    blk = lambda j, be, nu: (jnp.minimum(j, nu[0] - 1), 0)
    wsel = lambda j, be, nu: (be[jnp.minimum(j, nu[0] - 1)], 0, 0)
    grid_spec = pltpu.PrefetchScalarGridSpec(
        num_scalar_prefetch=2,
        grid=(nblk,),
        in_specs=[
            pl.BlockSpec((MOE_BLOCK, d), blk),
            pl.BlockSpec((1, d, d2), wsel),
            pl.BlockSpec((1, 1, d2), wsel),
            pl.BlockSpec((1, de, d), wsel),
            pl.BlockSpec((1, 1, d), wsel),
        ],
        out_specs=pl.BlockSpec((MOE_BLOCK, d), lambda j, be, nu: (j, 0)),
        scratch_shapes=[pltpu.VMEM((d, d2), BF16), pltpu.VMEM((de, d), BF16)],
    )
    return pl.pallas_call(
        _moe_body,
        out_shape=jax.ShapeDtypeStruct((cap, d), F32),
        grid_spec=grid_spec,
        compiler_params=_params(("arbitrary",)),
        name="moe_experts",
    )(block_expert, n_used, xb, w_gu, b_gu.reshape(ne, 1, d2), w_down, b_down.reshape(ne, 1, d))


def _combine_body(dest_ref, gt_ref, h1_ref, yb_hbm, g2_ref, b2_ref, o_ref, ybuf, sem, *, alpha):
    def row_copy(t, kk):
        return pltpu.make_async_copy(yb_hbm.at[pl.ds(dest_ref[0, 0, t * TOP_K + kk], 1)],
                                     ybuf.at[kk, pl.ds(t, 1)], sem)

    def issue(t, carry):
        for kk in range(TOP_K):
            row_copy(t, kk).start()
        return carry

    lax.fori_loop(0, COMBINE_TOKENS, issue, 0, unroll=DMA_UNROLL)

    def drain(t, carry):
        for kk in range(TOP_K):
            row_copy(t, kk).wait()
        return carry

    lax.fori_loop(0, COMBINE_TOKENS, drain, 0, unroll=DMA_UNROLL)

    gates = gt_ref[...]
    lane = lax.broadcasted_iota(jnp.int32, gates.shape, 1)
    ff = jnp.zeros(h1_ref.shape, F32)
    for kk in range(TOP_K):
        gk = jnp.sum(jnp.where(lane == kk, gates, 0.0), axis=1, keepdims=True)
        ff = ff + gk * ybuf[kk]
    o_ref[0] = _layer_norm(alpha * h1_ref[...] + ff, g2_ref[...], b2_ref[...])


def _combine(h1, yb, dest, gates, g2, b2, bsz, seq, alpha):
    n, d = h1.shape
    steps = n // COMBINE_TOKENS
    per_b = seq // COMBINE_TOKENS
    dest3 = dest.reshape(steps, 1, COMBINE_TOKENS * TOP_K)
    return pl.pallas_call(
        functools.partial(_combine_body, alpha=alpha),
        out_shape=jax.ShapeDtypeStruct((bsz, seq, d), F32),
        grid=(steps,),
        in_specs=[
            pl.BlockSpec((1, 1, COMBINE_TOKENS * TOP_K), lambda i: (i, 0, 0), memory_space=pltpu.SMEM),
            pl.BlockSpec((COMBINE_TOKENS, LANES), lambda i: (i, 0)),
            pl.BlockSpec((COMBINE_TOKENS, d), lambda i: (i, 0)),
            pl.BlockSpec(memory_space=pl.ANY),
            _const_spec((1, d)), _const_spec((1, d)),
        ],
        out_specs=pl.BlockSpec((1, COMBINE_TOKENS, d), lambda i: (i // per_b, i % per_b, 0)),
        scratch_shapes=[pltpu.VMEM((TOP_K, COMBINE_TOKENS, d), F32), pltpu.SemaphoreType.DMA(())],
        compiler_params=_params(("arbitrary",)),
        name="moe_combine_ln2",
    )(dest3, gates, h1, yb, g2, b2)


def _pad_cols(a, width):
    return jnp.pad(a, ((0, 0), (0, width - a.shape[1])))


def kernel(x, meta, ln0_g, ln0_b, w_in, b_fgate, fox_norm_g, rwkv_mu, w0, w2, a0, a2, g2, k_k, k_a, r_k, lnx_g, lnx_b, w_out, ln1_g, ln1_b, w_router, b_router, w_gu, b_gu, w_down, b_down, ln2_g, ln2_b):
    bsz, seq, d = x.shape
    depth = w_in.shape[0]
    assert depth == 1, "meta-token rows are dropped before the MoE, which is only valid for one layer"
    assert seq % TILE == 0 and seq % COMBINE_TOKENS == 0 and seq % DISPATCH_TOKENS == 0
    alpha = float((2 * depth) ** 0.25)
    n = bsz * seq
    ne = w_gu.shape[1]
    row = lambda a: a.reshape(1, -1)

    w = w_in[0]
    f3 = 3 * FOX_WIDTH
    r3 = 3 * RWKV_WIDTH
    wqkv = w[:, :f3].astype(BF16)
    wf = _pad_cols(w[:, f3:f3 + FOX_HEADS], LANES).astype(BF16)
    bf = _pad_cols(row(b_fgate[0]), LANES)
    wr_raw = w[:, f3 + FOX_HEADS:]

    def regroup(a):
        return jnp.concatenate([
            a[:, :r3],
            _pad_cols(a[:, r3:r3 + DECAY_LORA], LANES),
            _pad_cols(a[:, r3 + DECAY_LORA:r3 + DECAY_LORA + AAA_LORA], LANES),
            a[:, r3 + DECAY_LORA + AAA_LORA:],
        ], axis=1)

    wr = regroup(wr_raw).astype(BF16)
    mu = regroup(row(rwkv_mu[0]))
    w2p = jnp.pad(w2[0], ((0, LANES - DECAY_LORA), (0, 0))).astype(BF16)
    a2p = jnp.pad(a2[0], ((0, LANES - AAA_LORA), (0, 0))).astype(BF16)
    g2b = g2[0].astype(BF16)
    hid = jnp.arange(RWKV_WIDTH) // HEAD_DIM
    ones_bd = (hid[:, None] == hid[None, :]).astype(BF16)
    meta_tile = jnp.pad(meta.astype(x.dtype), ((TILE - N_META, 0), (0, 0)))

    qa, ka, va, zr = _inproj(x, meta_tile, row(ln0_g), row(ln0_b), wqkv, wf, bf, wr)
    o_fox = _fox_attention(qa, ka, va, row(fox_norm_g[0]))
    y_rwkv = _rwkv(zr, mu, row(w0[0]), w2p, row(a0[0]), a2p, g2b, row(k_k[0]), row(k_a[0]),
                   row(r_k[0]), row(lnx_g[0]), row(lnx_b[0]), ones_bd)

    wo = w_out[0].astype(BF16)
    w_rt = _pad_cols(w_router[0], LANES)
    b_rt = jnp.concatenate([row(b_router[0]), jnp.full((1, LANES - ne), NEG_BIG, F32)], axis=1)
    h1, e_pad, gates, rank_pad, counts_f = _outproj_router(
        x, o_fox, y_rwkv, row(ln0_g), row(ln0_b), wo[:FOX_WIDTH], wo[FOX_WIDTH:],
        row(ln1_g[0]), row(ln1_b[0]), w_rt, b_rt, alpha)

    counts = counts_f[0, :ne].astype(jnp.int32)
    pcounts = (counts + MOE_BLOCK - 1) // MOE_BLOCK * MOE_BLOCK
    pend = jnp.cumsum(pcounts)
    pstart = pend - pcounts
    nblk = n * TOP_K // MOE_BLOCK + ne
    cap = nblk * MOE_BLOCK
    dest = (pstart[e_pad[:, :TOP_K]] + rank_pad[:, :TOP_K]).reshape(-1)
    block_row = jnp.arange(nblk, dtype=jnp.int32) * MOE_BLOCK
    block_expert = jnp.minimum(
        jnp.sum((pend[None, :] <= block_row[:, None]).astype(jnp.int32), axis=1), ne - 1)
    n_used = (pend[-1:] // MOE_BLOCK).astype(jnp.int32)
    zstart = jnp.maximum(pend - MOE_BLOCK, 0).astype(jnp.int32)

    xb = _dispatch(h1, dest, zstart, n_used, cap)
    yb = _moe_experts(xb, block_expert, n_used, w_gu[0], b_gu[0], w_down[0], b_down[0])
    return _combine(h1, yb, dest, gates, row(ln2_g[0]), row(ln2_b[0]), bsz, seq, alpha)
```

```python
import functools
import math

import jax
import jax.numpy as jnp
from jax import lax
from jax.experimental import pallas as pl
from jax.experimental.pallas import tpu as pltpu

N_META = 16
HEAD_DIM = 64
FOX_HEADS = 8
RWKV_HEADS = 8
FOX_WIDTH = FOX_HEADS * HEAD_DIM
RWKV_WIDTH = RWKV_HEADS * HEAD_DIM
DECAY_LORA = 64
AAA_LORA = 64
GATE_LORA = 128
TOP_K = 4
SWIGLU_LIMIT = 7.0
SWIGLU_ALPHA = 1.702
LN_EPS = 1e-5
GN_EPS = 64e-5
RMS_EPS = 1e-6
NEG_BIG = -1e30
LOG2E = math.log2(math.e)

LANES = 128
MXU_DIM = 256
TILE = 512
CHUNK = 64
FOX_STEP_HEADS = 8
FOX_ROW_BLOCK = 128
MOE_BLOCK = 256
DISPATCH_TOKENS = 256
COMBINE_TOKENS = 256
DMA_UNROLL = 8
VMEM_LIMIT = 56 * 1024 * 1024

Q_BIAS_LANE = HEAD_DIM
K_BIAS_LANE = HEAD_DIM + 3
V_ONES_LANE = HEAD_DIM

F32 = jnp.float32
BF16 = jnp.bfloat16


def _dot(a, b):
    return jnp.dot(a, b, preferred_element_type=F32)


def _dot_nt(a, b):
    return lax.dot_general(a, b, (((1,), (1,)), ((), ())), preferred_element_type=F32)


def _dot_tn(a, b):
    return lax.dot_general(a, b, (((0,), (0,)), ((), ())), preferred_element_type=F32)


def _layer_norm(x, g, b):
    mu = jnp.mean(x, axis=-1, keepdims=True)
    xc = x - mu
    var = jnp.mean(xc * xc, axis=-1, keepdims=True)
    return xc * lax.rsqrt(var + LN_EPS) * g + b


def _softplus(x):
    return jnp.maximum(x, 0.0) + jnp.log1p(jnp.exp(-jnp.abs(x)))


def _row_cumsum(x, period, row):
    pos = row % period
    sh = 1
    while sh < period:
        x = x + jnp.where(pos >= sh, pltpu.roll(x, sh, 0), 0.0)
        sh *= 2
    return x


def _const_spec(shape):
    nd = len(shape)
    return pl.BlockSpec(shape, lambda *_: (0,) * nd)


def _params(sem):
    return pltpu.CompilerParams(dimension_semantics=sem, vmem_limit_bytes=VMEM_LIMIT)


def _inproj_body(x_ref, meta_ref, g_ref, b_ref, wqkv_ref, wf_ref, bf_ref, wr_ref,
                 qa_ref, ka_ref, va_ref, zr_ref, carry_s):
    i = pl.program_id(1)

    @pl.when(i == 0)
    def _reset():
        carry_s[...] = jnp.zeros(carry_s.shape, F32)

    x = jnp.where(i == 0, meta_ref[...], x_ref[0])
    h = _layer_norm(x, g_ref[...], b_ref[...])
    row = lax.broadcasted_iota(jnp.int32, (TILE, 1), 0)
    valid = jnp.logical_or(i > 0, row >= TILE - N_META)
    hb = jnp.where(valid, h, 0.0).astype(BF16)
    zr_ref[0] = _dot(hb, wr_ref[...]).astype(BF16)

    f = _dot(hb, wf_ref[...]) + bf_ref[...]
    lf = jnp.where(valid, -_softplus(-f), 0.0)
    cs = _row_cumsum(lf, TILE, row) + carry_s[...]
    carry_s[...] = cs[TILE - 1:TILE, :]
    c2 = cs * LOG2E

    qkv = _dot(hb, wqkv_ref[...])
    w = FOX_WIDTH
    qs = qkv[:, :w] * (LOG2E * HEAD_DIM ** -0.5)
    ks = qkv[:, w:2 * w]
    vs = qkv[:, 2 * w:]
    lane = lax.broadcasted_iota(jnp.int32, (TILE, LANES), 1)
    in_head = lane < HEAD_DIM
    for hd in range(FOX_HEADS):
        pair = slice(LANES * (hd // 2), LANES * (hd // 2 + 1))

        def head_lanes(a):
            slab = a[:, pair]
            return slab if hd % 2 == 0 else pltpu.roll(slab, HEAD_DIM, 1)

        col = jnp.broadcast_to(c2[:, hd:hd + 1], (TILE, LANES))
        hi = col.astype(BF16).astype(F32)
        rem = col - hi
        mid = rem.astype(BF16).astype(F32)
        lo = rem - mid
        q_bias = jnp.where(lane == Q_BIAS_LANE, hi, jnp.where(lane == Q_BIAS_LANE + 1, mid, lo))
        q_tail = jnp.where(lane < Q_BIAS_LANE + 3, q_bias, jnp.where(lane < Q_BIAS_LANE + 6, 1.0, 0.0))
        k_bias = jnp.where(lane == K_BIAS_LANE, -hi, jnp.where(lane == K_BIAS_LANE + 1, -mid, -lo))
        k_bias = jnp.where(valid, k_bias, jnp.where(lane == K_BIAS_LANE, NEG_BIG, 0.0))
        k_tail = jnp.where(lane < K_BIAS_LANE, 1.0, jnp.where(lane < K_BIAS_LANE + 3, k_bias, 0.0))
        qa_ref[0, hd] = jnp.where(in_head, head_lanes(qs), q_tail).astype(BF16)
        ka_ref[0, hd] = jnp.where(in_head, head_lanes(ks), k_tail).astype(BF16)
        va_ref[0, hd] = jnp.where(in_head, head_lanes(vs), jnp.where(lane == V_ONES_LANE, 1.0, 0.0)).astype(BF16)


def _inproj(x, meta_tile, g0, b0, wqkv, wf, bf, wr):
    bsz, seq, d = x.shape
    nt = seq // TILE + 1
    lp = nt * TILE
    rc = wr.shape[1]
    slab = jax.ShapeDtypeStruct((bsz, FOX_HEADS, lp, LANES), BF16)
    slab_spec = pl.BlockSpec((1, FOX_HEADS, TILE, LANES), lambda b, i: (b, 0, i, 0))
    return pl.pallas_call(
        _inproj_body,
        out_shape=(slab, slab, slab, jax.ShapeDtypeStruct((bsz, lp, rc), BF16)),
        grid=(bsz, nt),
        in_specs=[
            pl.BlockSpec((1, TILE, d), lambda b, i: (b, jnp.maximum(i - 1, 0), 0)),
            _const_spec((TILE, d)),
            _const_spec((1, d)), _const_spec((1, d)),
            _const_spec(wqkv.shape), _const_spec(wf.shape), _const_spec(bf.shape), _const_spec(wr.shape),
        ],
        out_specs=(slab_spec, slab_spec, slab_spec, pl.BlockSpec((1, TILE, rc), lambda b, i: (b, i, 0))),
        scratch_shapes=[pltpu.VMEM((1, LANES), F32)],
        compiler_params=_params(("parallel", "arbitrary")),
        name="ln0_inproj",
    )(x, meta_tile, g0, b0, wqkv, wf, bf, wr)


def _fox_body(qi_ref, ki_ref, q_ref, k_ref, v_ref, g_ref, o_ref, m_sc, acc_sc):
    t = pl.program_id(2)
    qi = qi_ref[t]
    ki = ki_ref[t]

    @pl.when(ki == 0)
    def _init():
        m_sc[...] = jnp.full(m_sc.shape, NEG_BIG, F32)
        acc_sc[...] = jnp.zeros(acc_sc.shape, F32)

    heads = range(FOX_STEP_HEADS)

    def scores(hh):
        return _dot_nt(q_ref[0, hh], k_ref[0, hh])

    def softmax_pv(hh, s, causal):
        for rb in range(TILE // FOX_ROW_BLOCK):
            rows = slice(rb * FOX_ROW_BLOCK, (rb + 1) * FOX_ROW_BLOCK)
            sb = s[rows]
            if causal:
                rr = lax.broadcasted_iota(jnp.int32, (FOX_ROW_BLOCK, TILE), 0) + rb * FOX_ROW_BLOCK
                cc = lax.broadcasted_iota(jnp.int32, (FOX_ROW_BLOCK, TILE), 1)
                sb = jnp.where(cc <= rr, sb, NEG_BIG)
            m_old = m_sc[hh, rows]
            m_col = jnp.maximum(m_old[:, 0:1], jnp.max(sb, axis=1, keepdims=True))
            m_new = jnp.broadcast_to(m_col, (FOX_ROW_BLOCK, LANES))
            p = jnp.exp2(sb - m_col).astype(BF16)
            acc_sc[hh, rows] = jnp.exp2(m_old - m_new) * acc_sc[hh, rows] + _dot(p, v_ref[0, hh])
            m_sc[hh, rows] = m_new

    def step(causal):
        s_next = scores(0)
        for hh in heads:
            s_cur = s_next
            if hh + 1 < FOX_STEP_HEADS:
                s_next = scores(hh + 1)
            softmax_pv(hh, s_cur, causal)

    @pl.when(ki < qi)
    def _off_diagonal():
        step(False)

    @pl.when(ki == qi)
    def _diagonal():
        step(True)
        lane = lax.broadcasted_iota(jnp.int32, (TILE, LANES), 1)
        lo = lane < HEAD_DIM
        normed = []
        for hh in heads:
            acc = acc_sc[hh]
            o = acc / acc[:, V_ONES_LANE:V_ONES_LANE + 1]
            ms = jnp.sum(jnp.where(lo, o * o, 0.0), axis=1, keepdims=True) * (1.0 / HEAD_DIM)
            normed.append(o * lax.rsqrt(ms + RMS_EPS))
        for pr in range(FOX_STEP_HEADS // 2):
            o = jnp.where(lo, normed[2 * pr], pltpu.roll(normed[2 * pr + 1], HEAD_DIM, 1))
            sl = slice(pr * LANES, (pr + 1) * LANES)
            o_ref[0, :, sl] = (o * g_ref[:, sl]).astype(BF16)


def _fox_attention(qa, ka, va, g):
    bsz, _, lp, _ = qa.shape
    nt = lp // TILE
    ngroup = FOX_HEADS // FOX_STEP_HEADS
    ow = FOX_STEP_HEADS * HEAD_DIM
    pairs = [(qi, ki) for qi in range(nt) for ki in range(qi + 1)]
    qi_tab = jnp.array([p[0] for p in pairs], jnp.int32)
    ki_tab = jnp.array([p[1] for p in pairs], jnp.int32)
    slab = (1, FOX_STEP_HEADS, TILE, LANES)
    kv_spec = pl.BlockSpec(slab, lambda b, p, t, qt, kt: (b, p, kt[t], 0))
    grid_spec = pltpu.PrefetchScalarGridSpec(
        num_scalar_prefetch=2,
        grid=(bsz, ngroup, len(pairs)),
        in_specs=[
            pl.BlockSpec(slab, lambda b, p, t, qt, kt: (b, p, qt[t], 0)),
            kv_spec, kv_spec,
            pl.BlockSpec((1, ow), lambda b, p, t, qt, kt: (0, p)),
        ],
        out_specs=pl.BlockSpec((1, TILE, ow), lambda b, p, t, qt, kt: (b, qt[t], p)),
        scratch_shapes=[pltpu.VMEM((FOX_STEP_HEADS, TILE, LANES), F32),
                        pltpu.VMEM((FOX_STEP_HEADS, TILE, LANES), F32)],
    )
    return pl.pallas_call(
        _fox_body,
        out_shape=jax.ShapeDtypeStruct((bsz, lp, FOX_WIDTH), BF16),
        grid_spec=grid_spec,
        compiler_params=_params(("parallel", "parallel", "arbitrary")),
        name="fox_attention",
    )(qi_tab, ki_tab, qa, ka, va, g)


def _interleave(*gens):
    gens = list(gens)
    while gens:
        for gen in list(gens):
            try:
                next(gen)
            except StopIteration:
                gens.remove(gen)


def _rwkv_body(z_ref, zp_ref, mu_ref, w0_ref, w2_ref, a0_ref, a2_ref, g2_ref, kk_ref, ka_ref, rk_ref,
               lg_ref, lb_ref, ones_ref, y_ref,
               ar_s, bt_s, kt_s, v_s, bkb_s, gam_s, s_s, y_s, bon_s, gate_s, low_s, arb_s, pv_s):
    i = pl.program_id(1)
    nch = TILE // CHUNK
    w = RWKV_WIDTH
    gw = MXU_DIM
    ngrp = w // gw

    @pl.when(i == 0)
    def _reset():
        s_s[...] = jnp.zeros(s_s.shape, F32)

    z = z_ref[0].astype(F32)
    prev = zp_ref[0][15:16, :].astype(F32)
    prev = jnp.where(i == 0, 0.0, prev)
    row = lax.broadcasted_iota(jnp.int32, (TILE, 1), 0)
    z_shift = jnp.where(row == 0, prev, pltpu.roll(z, 1, 0))
    zs = z + (z_shift - z) * mu_ref[...]
    r = zs[:, 0:w]
    kr = zs[:, w:2 * w]
    vr = zs[:, 2 * w:3 * w]
    dw = zs[:, 3 * w:3 * w + LANES]
    da = zs[:, 3 * w + LANES:3 * w + 2 * LANES]
    dg = zs[:, 3 * w + 2 * LANES:3 * w + 3 * LANES]
    lw = -math.exp(-0.5) * jax.nn.sigmoid(w0_ref[...] + _dot(jnp.tanh(dw).astype(BF16), w2_ref[...]))
    alpha = jax.nn.sigmoid(a0_ref[...] + _dot(da.astype(BF16), a2_ref[...]))
    gate_s[...] = _dot(jax.nn.sigmoid(dg).astype(BF16), g2_ref[...])
    ones = ones_ref[...]
    kk0 = kr * kk_ref[...]
    kk = kk0 * lax.rsqrt(jnp.maximum(_dot((kk0 * kk0).astype(BF16), ones), 1e-24))
    kmod = kr * (1.0 + (alpha - 1.0) * ka_ref[...])
    bon_s[...] = _dot((r * kmod * rk_ref[...]).astype(BF16), ones) * vr

    gcum = _row_cumsum(lw, CHUNK, row)
    glast = jnp.concatenate(
        [jnp.broadcast_to(gcum[c * CHUNK + CHUNK - 1:(c + 1) * CHUNK, :], (CHUNK, w)) for c in range(nch)], axis=0)
    e_out = jnp.exp(-gcum)
    e_tail = jnp.exp(glast - gcum)
    a_t = ((-kk) * jnp.exp(gcum - lw)).astype(BF16)
    r_t = (r * jnp.exp(gcum)).astype(BF16)
    bt_s[...] = (kk * alpha * e_out).astype(BF16)
    kt_s[...] = (kmod * e_out).astype(BF16)
    v_s[...] = vr.astype(BF16)
    b_bar = (kk * alpha * e_tail).astype(BF16)
    k_bar = (kmod * e_tail).astype(BF16)
    gam = jnp.exp(glast)
    for c in range(nch):
        rows = slice(c * CHUNK, (c + 1) * CHUNK)
        ar_s[c, 0:CHUNK, :] = a_t[rows]
        ar_s[c, CHUNK:2 * CHUNK, :] = r_t[rows]
        bkb_s[c, 0:CHUNK, :] = b_bar[rows]
        bkb_s[c, CHUNK:2 * CHUNK, :] = k_bar[rows]
        gam_s[c] = gam[c * CHUNK:c * CHUNK + 1]

    rb = lax.broadcasted_iota(jnp.int32, (gw, gw), 0)
    cb = lax.broadcasted_iota(jnp.int32, (gw, gw), 1)
    same_head = (rb // HEAD_DIM) == (cb // HEAD_DIM)
    r2 = lax.broadcasted_iota(jnp.int32, (2 * CHUNK, gw), 0)
    c2 = lax.broadcasted_iota(jnp.int32, (2 * CHUNK, gw), 1) % CHUNK
    tri2 = c2 < jnp.where(r2 < CHUNK, r2, r2 - CHUNK + 1)
    r1 = lax.broadcasted_iota(jnp.int32, (CHUNK, gw), 0)
    c1 = lax.broadcasted_iota(jnp.int32, (CHUNK, gw), 1) % CHUNK
    level_masks = []
    s = 1
    while s < CHUNK:
        level_masks.append(jnp.logical_and(((r1 ^ c1) & (-s)) == s, r1 > c1))
        s *= 2

    def block_diag(x):
        xb = x.astype(BF16)
        return jnp.where(same_head, jnp.concatenate([xb] * (gw // CHUNK), axis=0), jnp.zeros((), BF16))

    def lanes(g):
        return slice(g * gw, (g + 1) * gw)

    def intra_chunk(chunks):
        keys = [(c, g) for c in chunks for g in range(ngrp)]
        n_ab, low = {}, {}
        for c, g in keys:
            rows = slice(c * CHUNK, (c + 1) * CHUNK)
            ar = ar_s[c, :, lanes(g)]
            pb = jnp.where(tri2, _dot_nt(ar, block_diag(bt_s[rows, lanes(g)])), 0.0)
            pk = jnp.where(tri2, _dot_nt(ar, block_diag(kt_s[rows, lanes(g)])), 0.0)
            n_ab[c, g] = pb[0:CHUNK]
            arb_s[c, :, lanes(g)] = pb[CHUNK:2 * CHUNK].astype(BF16)
            pv_s[c, :, lanes(g)] = _dot(pk.astype(BF16), block_diag(v_s[rows, lanes(g)]))
            low[c, g] = jnp.where(level_masks[0], n_ab[c, g], 0.0)
        yield
        for m in level_masks[1:]:
            x = {}
            for key in keys:
                n_off = jnp.where(m, n_ab[key], 0.0)
                x[key] = n_off + _dot(n_off.astype(BF16), block_diag(low[key]))
            yield
            for key in keys:
                low[key] = low[key] + x[key] + _dot(low[key].astype(BF16), block_diag(x[key]))
            yield
        for c, g in keys:
            low_s[c, :, lanes(g)] = low[c, g].astype(BF16)

    def inter_chunk(chunks):
        for c in chunks:
            rows = slice(c * CHUNK, (c + 1) * CHUNK)
            st, ars, u = {}, {}, {}
            for g in range(ngrp):
                st[g] = s_s[g]
                ars[g] = _dot_nt(ar_s[c, :, lanes(g)], st[g].astype(BF16))
            yield
            for g in range(ngrp):
                wm = ars[g][0:CHUNK] + pv_s[c, 0:CHUNK, lanes(g)]
                u[g] = wm + _dot(low_s[c, :, lanes(g)], block_diag(wm))
            yield
            for g in range(ngrp):
                y_s[rows, lanes(g)] = (ars[g][CHUNK:2 * CHUNK] + pv_s[c, CHUNK:2 * CHUNK, lanes(g)]
                                       + _dot(arb_s[c, :, lanes(g)], block_diag(u[g])))
                uv = jnp.concatenate([u[g].astype(BF16), v_s[rows, lanes(g)]], axis=0)
                upd = _dot_tn(uv, bkb_s[c, :, lanes(g)])
                s_s[g] = st[g] * gam_s[c][:, lanes(g)] + jnp.where(same_head, upd, 0.0)
            yield

    half = nch // 2
    _interleave(intra_chunk(range(0, half)))
    _interleave(intra_chunk(range(half, nch)), inter_chunk(range(0, half)))
    _interleave(inter_chunk(range(half, nch)))

    y = y_s[...]
    mean = _dot(y.astype(BF16), ones) * (1.0 / HEAD_DIM)
    d = y - mean
    var = _dot((d * d).astype(BF16), ones) * (1.0 / HEAD_DIM)
    yn = d * lax.rsqrt(var + GN_EPS)
    y_ref[0] = ((yn * lg_ref[...] + lb_ref[...] + bon_s[...]) * gate_s[...]).astype(BF16)


def _rwkv(zr, mu, w0, w2, a0, a2, g2, k_k, k_a, r_k, lnx_g, lnx_b, ones_bd):
    bsz, lp, rc = zr.shape
    nt = lp // TILE
    nch = TILE // CHUNK
    w = RWKV_WIDTH
    vec = _const_spec((1, w))
    return pl.pallas_call(
        _rwkv_body,
        out_shape=jax.ShapeDtypeStruct((bsz, lp, w), BF16),
        grid=(bsz, nt),
        in_specs=[
            pl.BlockSpec((1, TILE, rc), lambda b, i: (b, i, 0)),
            pl.BlockSpec((1, 16, rc), lambda b, i: (b, jnp.maximum(i * (TILE // 16) - 1, 0), 0)),
            _const_spec((1, rc)),
            vec, _const_spec(w2.shape), vec, _const_spec(a2.shape), _const_spec(g2.shape),
            vec, vec, vec, vec, vec,
            _const_spec(ones_bd.shape),
        ],
        out_specs=pl.BlockSpec((1, TILE, w), lambda b, i: (b, i, 0)),
        scratch_shapes=[
            pltpu.VMEM((nch, 2 * CHUNK, w), BF16),
            pltpu.VMEM((TILE, w), BF16),
            pltpu.VMEM((TILE, w), BF16),
            pltpu.VMEM((TILE, w), BF16),
            pltpu.VMEM((nch, 2 * CHUNK, w), BF16),
            pltpu.VMEM((nch, 1, w), F32),
            pltpu.VMEM((w // MXU_DIM, MXU_DIM, MXU_DIM), F32),
            pltpu.VMEM((TILE, w), F32),
            pltpu.VMEM((TILE, w), F32),
            pltpu.VMEM((TILE, w), F32),
            pltpu.VMEM((nch, CHUNK, w), BF16),
            pltpu.VMEM((nch, CHUNK, w), BF16),
            pltpu.VMEM((nch, 2 * CHUNK, w), F32),
        ],
        compiler_params=_params(("parallel", "arbitrary")),
        name="rwkv7_mix",
    )(zr, zr, mu, w0, w2, a0, a2, g2, k_k, k_a, r_k, lnx_g, lnx_b, ones_bd)


def _outproj_body(x_ref, of_ref, yr_ref, g0_ref, b0_ref, wo1_ref, wo2_ref, g1_ref, b1_ref, wrh_ref, wrl_ref, br_ref,
                  h1_ref, e_ref, gt_ref, rk_ref, cnt_ref, carry_s, *, alpha):
    first = jnp.logical_and(pl.program_id(0) == 0, pl.program_id(1) == 0)

    @pl.when(first)
    def _reset():
        carry_s[...] = jnp.zeros(carry_s.shape, F32)

    h0 = _layer_norm(x_ref[0], g0_ref[...], b0_ref[...])
    mix = _dot(of_ref[0], wo1_ref[...]) + _dot(yr_ref[0], wo2_ref[...])
    h1 = _layer_norm(alpha * h0 + mix, g1_ref[...], b1_ref[...])
    h1_ref[...] = h1
    h_hi = h1.astype(BF16)
    h_lo = (h1 - h_hi.astype(F32)).astype(BF16)
    logits = ((_dot(h_hi, wrh_ref[...]) + _dot(h_lo, wrl_ref[...]))
              + (_dot(h_hi, wrl_ref[...]) + _dot(h_lo, wrh_ref[...]))) + br_ref[...]
    lane = lax.broadcasted_iota(jnp.int32, (TILE, LANES), 1)
    lane_f = lane.astype(F32)
    cur = logits
    vals, idxs = [], []
    for _ in range(TOP_K):
        m = jnp.max(cur, axis=1, keepdims=True)
        idx = jnp.min(jnp.where(cur == m, lane_f, float(LANES)), axis=1, keepdims=True).astype(jnp.int32)
        vals.append(m)
        idxs.append(idx)
        cur = jnp.where(lane == idx, -jnp.inf, cur)
    exps = [jnp.exp(v - vals[0]) for v in vals]
    denom = exps[0] + exps[1] + exps[2] + exps[3]
    hot = [lane == idx for idx in idxs]
    multi = jnp.zeros((TILE, LANES), F32)
    for hk in hot:
        multi = multi + hk.astype(F32)
    rr = lax.broadcasted_iota(jnp.int32, (TILE, TILE), 0)
    cc = lax.broadcasted_iota(jnp.int32, (TILE, TILE), 1)
    before = _dot((cc < rr).astype(BF16), multi.astype(BF16)) + carry_s[...]
    e_out = jnp.zeros((TILE, LANES), jnp.int32)
    g_out = jnp.zeros((TILE, LANES), F32)
    r_out = jnp.zeros((TILE, LANES), F32)
    for kk in range(TOP_K):
        rank = jnp.sum(jnp.where(hot[kk], before, 0.0), axis=1, keepdims=True)
        e_out = jnp.where(lane == kk, idxs[kk], e_out)
        g_out = jnp.where(lane == kk, exps[kk] / denom, g_out)
        r_out = jnp.where(lane == kk, rank, r_out)
    e_ref[...] = e_out
    gt_ref[...] = g_out
    rk_ref[...] = r_out.astype(jnp.int32)
    carry_s[...] = carry_s[...] + jnp.sum(multi, axis=0, keepdims=True)
    cnt_ref[...] = carry_s[...]


def _outproj_router(x, o_fox, y_rwkv, g0, b0, wo1, wo2, g1, b1, w_router_hi, w_router_lo, b_router, alpha):
    bsz, seq, d = x.shape
    nt = seq // TILE
    n = bsz * seq
    row_blk = lambda c: pl.BlockSpec((TILE, c), lambda b, i: (b * nt + i, 0))
    mix_spec = pl.BlockSpec((1, TILE, FOX_WIDTH), lambda b, i: (b, i + 1, 0))
    return pl.pallas_call(
        functools.partial(_outproj_body, alpha=alpha),
        out_shape=(jax.ShapeDtypeStruct((n, d), F32),
                   jax.ShapeDtypeStruct((n, LANES), jnp.int32),
                   jax.ShapeDtypeStruct((n, LANES), F32),
                   jax.ShapeDtypeStruct((n, LANES), jnp.int32),
                   jax.ShapeDtypeStruct((1, LANES), F32)),
        grid=(bsz, nt),
        in_specs=[
            pl.BlockSpec((1, TILE, d), lambda b, i: (b, i, 0)),
            mix_spec, mix_spec,
            _const_spec((1, d)), _const_spec((1, d)),
            _const_spec(wo1.shape), _const_spec(wo2.shape),
            _const_spec((1, d)), _const_spec((1, d)),
            _const_spec(w_router_hi.shape), _const_spec(w_router_lo.shape), _const_spec(b_router.shape),
        ],
        out_specs=(row_blk(d), row_blk(LANES), row_blk(LANES), row_blk(LANES), _const_spec((1, LANES))),
        scratch_shapes=[pltpu.VMEM((1, LANES), F32)],
        compiler_params=_params(("arbitrary", "arbitrary")),
        name="outproj_ln1_router",
    )(x, o_fox, y_rwkv, g0, b0, wo1, wo2, g1, b1, w_router_hi, w_router_lo, b_router)


def _dispatch_body(dest_ref, zstart_ref, nused_ref, h1_ref, xb_hbm, zero_buf, sem, zsem, *, n_experts, n_blocks):
    i = pl.program_id(0)

    def zero_copy(start):
        start = pl.multiple_of(start, MOE_BLOCK)
        return pltpu.make_async_copy(zero_buf, xb_hbm.at[pl.ds(start, MOE_BLOCK)], zsem)

    @pl.when(i == 0)
    def _clear():
        zero_buf[...] = jnp.zeros(zero_buf.shape, F32)
        for e in range(n_experts):
            zero_copy(zstart_ref[e]).start()

        def start_tail(j, carry):
            zero_copy(j * MOE_BLOCK).start()
            return carry

        lax.fori_loop(nused_ref[0], n_blocks, start_tail, 0)
        for e in range(n_experts):
            zero_copy(zstart_ref[e]).wait()

        def wait_tail(j, carry):
            zero_copy(j * MOE_BLOCK).wait()
            return carry

        lax.fori_loop(nused_ref[0], n_blocks, wait_tail, 0)

    def row_copy(t, kk):
        return pltpu.make_async_copy(h1_ref.at[pl.ds(t, 1)],
                                     xb_hbm.at[pl.ds(dest_ref[0, 0, t * TOP_K + kk], 1)], sem)

    def issue(t, carry):
        for kk in range(TOP_K):
            row_copy(t, kk).start(priority=kk % 2)
        return carry

    lax.fori_loop(0, DISPATCH_TOKENS, issue, 0, unroll=DMA_UNROLL)

    def drain(t, carry):
        for kk in range(TOP_K):
            row_copy(t, kk).wait()
        return carry

    lax.fori_loop(0, DISPATCH_TOKENS, drain, 0, unroll=DMA_UNROLL)


def _dispatch(h1, dest, zstart, n_used, cap):
    n, d = h1.shape
    steps = n // DISPATCH_TOKENS
    dest3 = dest.reshape(steps, 1, DISPATCH_TOKENS * TOP_K)
    return pl.pallas_call(
        functools.partial(_dispatch_body, n_experts=zstart.shape[0], n_blocks=cap // MOE_BLOCK),
        out_shape=jax.ShapeDtypeStruct((cap, d), F32),
        grid=(steps,),
        in_specs=[
            pl.BlockSpec((1, 1, DISPATCH_TOKENS * TOP_K), lambda i: (i, 0, 0), memory_space=pltpu.SMEM),
            pl.BlockSpec(memory_space=pltpu.SMEM),
            pl.BlockSpec(memory_space=pltpu.SMEM),
            pl.BlockSpec((DISPATCH_TOKENS, d), lambda i: (i, 0)),
        ],
        out_specs=pl.BlockSpec(memory_space=pl.ANY),
        scratch_shapes=[pltpu.VMEM((MOE_BLOCK, d), F32), pltpu.SemaphoreType.DMA(()), pltpu.SemaphoreType.DMA(())],
        compiler_params=_params(("arbitrary",)),
        name="moe_dispatch",
    )(dest3, zstart, n_used, h1)


def _moe_body(be_ref, nused_ref, xb_ref, wgu_ref, bgu_ref, wd_ref, bd_ref, yb_ref, wgu_bf, wd_bf):
    j = pl.program_id(0)

    @pl.when(j < nused_ref[0])
    def _run():
        prev = be_ref[jnp.maximum(j - 1, 0)]
        changed = jnp.logical_or(j == 0, be_ref[j] != prev)

        @pl.when(changed)
        def _cast():
            wgu_bf[...] = wgu_ref[0].astype(BF16)
            wd_bf[...] = wd_ref[0].astype(BF16)

        de = wd_bf.shape[0]
        gu = _dot(xb_ref[...].astype(BF16), wgu_bf[...]) + bgu_ref[0]
        gate = jnp.minimum(gu[:, :de], SWIGLU_LIMIT)
        up = jnp.clip(gu[:, de:], -SWIGLU_LIMIT, SWIGLU_LIMIT)
        act = (up + 1.0) * (gate * jax.nn.sigmoid(gate * SWIGLU_ALPHA))
        yb_ref[...] = _dot(act.astype(BF16), wd_bf[...]) + bd_ref[0]

    @pl.when(j >= nused_ref[0])
    def _unused():
        yb_ref[...] = jnp.zeros(yb_ref.shape, F32)


def _moe_experts(xb, block_expert, n_used, w_gu, b_gu, w_down, b_down):
    cap, d = xb.shape
    ne, _, d2 = w_gu.shape
    de = w_down.shape[1]
    nblk = cap // MOE_BLOCK
    blk = lambda j, be, nu: (jnp.minimum(j, nu[0] - 1), 0)
    wsel = lambda j, be, nu: (be[jnp.minimum(j, nu[0] - 1)], 0, 0)
    grid_spec = pltpu.PrefetchScalarGridSpec(
        num_scalar_prefetch=2,
        grid=(nblk,),
        in_specs=[
            pl.BlockSpec((MOE_BLOCK, d), blk),
            pl.BlockSpec((1, d, d2), wsel),
            pl.BlockSpec((1, 1, d2), wsel),
            pl.BlockSpec((1, de, d), wsel),
            pl.BlockSpec((1, 1, d), wsel),
        ],
        out_specs=pl.BlockSpec((MOE_BLOCK, d), lambda j, be, nu: (j, 0)),
        scratch_shapes=[pltpu.VMEM((d, d2), BF16), pltpu.VMEM((de, d), BF16)],
    )
    return pl.pallas_call(
        _moe_body,
        out_shape=jax.ShapeDtypeStruct((cap, d), F32),
        grid_spec=grid_spec,
        compiler_params=_params(("arbitrary",)),
        name="moe_experts",
    )(block_expert, n_used, xb, w_gu, b_gu.reshape(ne, 1, d2), w_down, b_down.reshape(ne, 1, d))


def _combine_body(dest_ref, dest_next_ref, gt_ref, h1_ref, yb_hbm, g2_ref, b2_ref, o_ref, ybuf, sems, *, alpha):
    i = pl.program_id(0)
    slot = i % 2

    def row_copy(dref, s, t, kk):
        return pltpu.make_async_copy(yb_hbm.at[pl.ds(dref[0, 0, t * TOP_K + kk], 1)],
                                     ybuf.at[s, kk, pl.ds(t, 1)], sems.at[s])

    def issue(dref, s):
        def body(t, carry):
            for kk in range(TOP_K):
                row_copy(dref, s, t, kk).start(priority=kk % 2)
            return carry

        lax.fori_loop(0, COMBINE_TOKENS, body, 0, unroll=DMA_UNROLL)

    @pl.when(i == 0)
    def _prologue():
        issue(dest_ref, 0)

    @pl.when(i + 1 < pl.num_programs(0))
    def _prefetch():
        issue(dest_next_ref, 1 - slot)

    def drain(t, carry):
        for kk in range(TOP_K):
            row_copy(dest_ref, slot, t, kk).wait()
        return carry

    lax.fori_loop(0, COMBINE_TOKENS, drain, 0, unroll=DMA_UNROLL)

    gates = gt_ref[...]
    lane = lax.broadcasted_iota(jnp.int32, gates.shape, 1)
    ff = jnp.zeros(h1_ref.shape, F32)
    for kk in range(TOP_K):
        gk = jnp.sum(jnp.where(lane == kk, gates, 0.0), axis=1, keepdims=True)
        ff = ff + gk * ybuf[slot, kk]
    o_ref[0] = _layer_norm(alpha * h1_ref[...] + ff, g2_ref[...], b2_ref[...])


def _combine(h1, yb, dest, gates, g2, b2, bsz, seq, alpha):
    n, d = h1.shape
    steps = n // COMBINE_TOKENS
    per_b = seq // COMBINE_TOKENS
    dest3 = dest.reshape(steps, 1, COMBINE_TOKENS * TOP_K)
    dest_blk = (1, 1, COMBINE_TOKENS * TOP_K)
    return pl.pallas_call(
        functools.partial(_combine_body, alpha=alpha),
        out_shape=jax.ShapeDtypeStruct((bsz, seq, d), F32),
        grid=(steps,),
        in_specs=[
            pl.BlockSpec(dest_blk, lambda i: (i, 0, 0), memory_space=pltpu.SMEM),
            pl.BlockSpec(dest_blk, lambda i: (jnp.minimum(i + 1, steps - 1), 0, 0), memory_space=pltpu.SMEM),
            pl.BlockSpec((COMBINE_TOKENS, LANES), lambda i: (i, 0)),
            pl.BlockSpec((COMBINE_TOKENS, d), lambda i: (i, 0)),
            pl.BlockSpec(memory_space=pl.ANY),
            _const_spec((1, d)), _const_spec((1, d)),
        ],
        out_specs=pl.BlockSpec((1, COMBINE_TOKENS, d), lambda i: (i // per_b, i % per_b, 0)),
        scratch_shapes=[pltpu.VMEM((2, TOP_K, COMBINE_TOKENS, d), F32), pltpu.SemaphoreType.DMA((2,))],
        compiler_params=_params(("arbitrary",)),
        name="moe_combine_ln2",
    )(dest3, dest3, gates, h1, yb, g2, b2)


def _pad_cols(a, width):
    return jnp.pad(a, ((0, 0), (0, width - a.shape[1])))


def kernel(x, meta, ln0_g, ln0_b, w_in, b_fgate, fox_norm_g, rwkv_mu, w0, w2, a0, a2, g2, k_k, k_a, r_k, lnx_g, lnx_b, w_out, ln1_g, ln1_b, w_router, b_router, w_gu, b_gu, w_down, b_down, ln2_g, ln2_b):
    bsz, seq, d = x.shape
    depth = w_in.shape[0]
    assert depth == 1, "meta-token rows are dropped before the MoE, which is only valid for one layer"
    assert seq % TILE == 0 and seq % COMBINE_TOKENS == 0 and seq % DISPATCH_TOKENS == 0
    alpha = float((2 * depth) ** 0.25)
    n = bsz * seq
    ne = w_gu.shape[1]
    row = lambda a: a.reshape(1, -1)

    w = w_in[0]
    f3 = 3 * FOX_WIDTH
    r3 = 3 * RWKV_WIDTH
    wqkv = w[:, :f3].astype(BF16)
    wf = _pad_cols(w[:, f3:f3 + FOX_HEADS], LANES).astype(BF16)
    bf = _pad_cols(row(b_fgate[0]), LANES)
    wr_raw = w[:, f3 + FOX_HEADS:]

    def regroup(a):
        return jnp.concatenate([
            a[:, :r3],
            _pad_cols(a[:, r3:r3 + DECAY_LORA], LANES),
            _pad_cols(a[:, r3 + DECAY_LORA:r3 + DECAY_LORA + AAA_LORA], LANES),
            a[:, r3 + DECAY_LORA + AAA_LORA:],
        ], axis=1)

    wr = regroup(wr_raw).astype(BF16)
    mu = regroup(row(rwkv_mu[0]))
    w2p = jnp.pad(w2[0], ((0, LANES - DECAY_LORA), (0, 0))).astype(BF16)
    a2p = jnp.pad(a2[0], ((0, LANES - AAA_LORA), (0, 0))).astype(BF16)
    g2b = g2[0].astype(BF16)
    hid = jnp.arange(RWKV_WIDTH) // HEAD_DIM
    ones_bd = (hid[:, None] == hid[None, :]).astype(BF16)
    meta_tile = jnp.pad(meta.astype(x.dtype), ((TILE - N_META, 0), (0, 0)))

    qa, ka, va, zr = _inproj(x, meta_tile, row(ln0_g), row(ln0_b), wqkv, wf, bf, wr)
    o_fox = _fox_attention(qa, ka, va, row(fox_norm_g[0]))
    y_rwkv = _rwkv(zr, mu, row(w0[0]), w2p, row(a0[0]), a2p, g2b, row(k_k[0]), row(k_a[0]),
                   row(r_k[0]), row(lnx_g[0]), row(lnx_b[0]), ones_bd)

    wo = w_out[0].astype(BF16)
    w_rt = _pad_cols(w_router[0], LANES)
    w_rt_hi = w_rt.astype(BF16)
    w_rt_lo = (w_rt - w_rt_hi.astype(F32)).astype(BF16)
    b_rt = jnp.concatenate([row(b_router[0]), jnp.full((1, LANES - ne), NEG_BIG, F32)], axis=1)
    h1, e_pad, gates, rank_pad, counts_f = _outproj_router(
        x, o_fox, y_rwkv, row(ln0_g), row(ln0_b), wo[:FOX_WIDTH], wo[FOX_WIDTH:],
        row(ln1_g[0]), row(ln1_b[0]), w_rt_hi, w_rt_lo, b_rt, alpha)

    counts = counts_f[0, :ne].astype(jnp.int32)
    pcounts = (counts + MOE_BLOCK - 1) // MOE_BLOCK * MOE_BLOCK
    pend = jnp.cumsum(pcounts)
    pstart = pend - pcounts
    nblk = n * TOP_K // MOE_BLOCK + ne
    cap = nblk * MOE_BLOCK
    dest = (pstart[e_pad[:, :TOP_K]] + rank_pad[:, :TOP_K]).reshape(-1)
    block_row = jnp.arange(nblk, dtype=jnp.int32) * MOE_BLOCK
    block_expert = jnp.minimum(
        jnp.sum((pend[None, :] <= block_row[:, None]).astype(jnp.int32), axis=1), ne - 1)
    n_used = (pend[-1:] // MOE_BLOCK).astype(jnp.int32)
    zstart = jnp.maximum(pend - MOE_BLOCK, 0).astype(jnp.int32)

    xb = _dispatch(h1, dest, zstart, n_used, cap)
    yb = _moe_experts(xb, block_expert, n_used, w_gu[0], b_gu[0], w_down[0], b_down[0])
    return _combine(h1, yb, dest, gates, row(ln2_g[0]), row(ln2_b[0]), bsz, seq, alpha)
```

```python
import functools
import math

import jax
import jax.numpy as jnp
from jax import lax
from jax.experimental import pallas as pl
from jax.experimental.pallas import tpu as pltpu

N_META = 16
HEAD_DIM = 64
FOX_HEADS = 8
RWKV_HEADS = 8
FOX_WIDTH = FOX_HEADS * HEAD_DIM
RWKV_WIDTH = RWKV_HEADS * HEAD_DIM
DECAY_LORA = 64
AAA_LORA = 64
GATE_LORA = 128
TOP_K = 4
SWIGLU_LIMIT = 7.0
SWIGLU_ALPHA = 1.702
LN_EPS = 1e-5
GN_EPS = 64e-5
RMS_EPS = 1e-6
NEG_BIG = -1e30
LOG2E = math.log2(math.e)

LANES = 128
MXU_DIM = 256
TILE = 512
CHUNK = 64
FOX_ROW_BLOCK = 256
MOE_BLOCK = 256
DISPATCH_TOKENS = 256
COMBINE_TOKENS = 256
DMA_UNROLL = 8
VMEM_LIMIT = 56 * 1024 * 1024

Q_BIAS_LANE = HEAD_DIM
K_BIAS_LANE = HEAD_DIM + 3
V_ONES_LANE = HEAD_DIM

F32 = jnp.float32
BF16 = jnp.bfloat16


def _dot(a, b):
    return jnp.dot(a, b, preferred_element_type=F32)


def _dot_nt(a, b):
    return lax.dot_general(a, b, (((1,), (1,)), ((), ())), preferred_element_type=F32)


def _dot_tn(a, b):
    return lax.dot_general(a, b, (((0,), (0,)), ((), ())), preferred_element_type=F32)


def _layer_norm(x, g, b):
    mu = jnp.mean(x, axis=-1, keepdims=True)
    xc = x - mu
    var = jnp.mean(xc * xc, axis=-1, keepdims=True)
    return xc * lax.rsqrt(var + LN_EPS) * g + b


def _softplus(x):
    return jnp.maximum(x, 0.0) + jnp.log1p(jnp.exp(-jnp.abs(x)))


def _row_cumsum(x, period, row):
    pos = row % period
    sh = 1
    while sh < period:
        x = x + jnp.where(pos >= sh, pltpu.roll(x, sh, 0), 0.0)
        sh *= 2
    return x


def _const_spec(shape):
    nd = len(shape)
    return pl.BlockSpec(shape, lambda *_: (0,) * nd)


def _params(sem):
    return pltpu.CompilerParams(dimension_semantics=sem, vmem_limit_bytes=VMEM_LIMIT)


def _inproj_body(x_ref, meta_ref, g_ref, b_ref, wqkv_ref, wf_ref, bf_ref, wr_ref,
                 qa_ref, ka_ref, va_ref, zr_ref, carry_s):
    i = pl.program_id(1)

    @pl.when(i == 0)
    def _reset():
        carry_s[...] = jnp.zeros(carry_s.shape, F32)

    x = jnp.where(i == 0, meta_ref[...], x_ref[0])
    h = _layer_norm(x, g_ref[...], b_ref[...])
    row = lax.broadcasted_iota(jnp.int32, (TILE, 1), 0)
    valid = jnp.logical_or(i > 0, row >= TILE - N_META)
    hb = jnp.where(valid, h, 0.0).astype(BF16)
    zr_ref[0] = _dot(hb, wr_ref[...]).astype(BF16)

    f = _dot(hb, wf_ref[...]) + bf_ref[...]
    lf = jnp.where(valid, -_softplus(-f), 0.0)
    cs = _row_cumsum(lf, TILE, row) + carry_s[...]
    carry_s[...] = cs[TILE - 1:TILE, :]
    c2 = cs * LOG2E

    qkv = _dot(hb, wqkv_ref[...])
    w = FOX_WIDTH
    qs = qkv[:, :w] * (LOG2E * HEAD_DIM ** -0.5)
    ks = qkv[:, w:2 * w]
    vs = qkv[:, 2 * w:]
    lane = lax.broadcasted_iota(jnp.int32, (TILE, LANES), 1)
    in_head = lane < HEAD_DIM
    for hd in range(FOX_HEADS):
        pair = slice(LANES * (hd // 2), LANES * (hd // 2 + 1))

        def head_lanes(a):
            slab = a[:, pair]
            return slab if hd % 2 == 0 else pltpu.roll(slab, HEAD_DIM, 1)

        col = jnp.broadcast_to(c2[:, hd:hd + 1], (TILE, LANES))
        hi = col.astype(BF16).astype(F32)
        rem = col - hi
        mid = rem.astype(BF16).astype(F32)
        lo = rem - mid
        q_bias = jnp.where(lane == Q_BIAS_LANE, hi, jnp.where(lane == Q_BIAS_LANE + 1, mid, lo))
        q_tail = jnp.where(lane < Q_BIAS_LANE + 3, q_bias, jnp.where(lane < Q_BIAS_LANE + 6, 1.0, 0.0))
        k_bias = jnp.where(lane == K_BIAS_LANE, -hi, jnp.where(lane == K_BIAS_LANE + 1, -mid, -lo))
        k_bias = jnp.where(valid, k_bias, jnp.where(lane == K_BIAS_LANE, NEG_BIG, 0.0))
        k_tail = jnp.where(lane < K_BIAS_LANE, 1.0, jnp.where(lane < K_BIAS_LANE + 3, k_bias, 0.0))
        qa_ref[0, hd] = jnp.where(in_head, head_lanes(qs), q_tail).astype(BF16)
        ka_ref[0, hd] = jnp.where(in_head, head_lanes(ks), k_tail).astype(BF16)
        va_ref[0, hd] = jnp.where(in_head, head_lanes(vs), jnp.where(lane == V_ONES_LANE, 1.0, 0.0)).astype(BF16)


def _inproj(x, meta_tile, g0, b0, wqkv, wf, bf, wr):
    bsz, seq, d = x.shape
    nt = seq // TILE + 1
    lp = nt * TILE
    rc = wr.shape[1]
    slab = jax.ShapeDtypeStruct((bsz, FOX_HEADS, lp, LANES), BF16)
    slab_spec = pl.BlockSpec((1, FOX_HEADS, TILE, LANES), lambda b, i: (b, 0, i, 0))
    return pl.pallas_call(
        _inproj_body,
        out_shape=(slab, slab, slab, jax.ShapeDtypeStruct((bsz, lp, rc), BF16)),
        grid=(bsz, nt),
        in_specs=[
            pl.BlockSpec((1, TILE, d), lambda b, i: (b, jnp.maximum(i - 1, 0), 0)),
            _const_spec((TILE, d)),
            _const_spec((1, d)), _const_spec((1, d)),
            _const_spec(wqkv.shape), _const_spec(wf.shape), _const_spec(bf.shape), _const_spec(wr.shape),
        ],
        out_specs=(slab_spec, slab_spec, slab_spec, pl.BlockSpec((1, TILE, rc), lambda b, i: (b, i, 0))),
        scratch_shapes=[pltpu.VMEM((1, LANES), F32)],
        compiler_params=_params(("parallel", "arbitrary")),
        name="ln0_inproj",
    )(x, meta_tile, g0, b0, wqkv, wf, bf, wr)


def _fox_body(q_ref, k_hbm, v_hbm, g_ref, o_ref, m_sc, acc_sc, kbuf, vbuf, ksem, vsem):
    b = pl.program_id(0)
    qi = pl.program_id(1)

    def kv_copy(ki, slot):
        rows = pl.ds(pl.multiple_of(ki * TILE, TILE), TILE)
        return (pltpu.make_async_copy(k_hbm.at[b, :, rows, :], kbuf.at[slot], ksem.at[slot]),
                pltpu.make_async_copy(v_hbm.at[b, :, rows, :], vbuf.at[slot], vsem.at[slot]))

    def start(ki, slot):
        for cp in kv_copy(ki, slot):
            cp.start()

    def wait(ki, slot):
        for cp in kv_copy(ki, slot):
            cp.wait()

    start(0, 0)
    m_sc[...] = jnp.full(m_sc.shape, NEG_BIG, F32)
    acc_sc[...] = jnp.zeros(acc_sc.shape, F32)

    heads = range(FOX_HEADS)

    def scores(hh, slot):
        return _dot_nt(q_ref[0, hh], kbuf[slot, hh])

    def softmax_pv(hh, s, causal, slot):
        for rb in range(TILE // FOX_ROW_BLOCK):
            rows = slice(rb * FOX_ROW_BLOCK, (rb + 1) * FOX_ROW_BLOCK)
            sb = s[rows]
            if causal:
                rr = lax.broadcasted_iota(jnp.int32, (FOX_ROW_BLOCK, TILE), 0) + rb * FOX_ROW_BLOCK
                cc = lax.broadcasted_iota(jnp.int32, (FOX_ROW_BLOCK, TILE), 1)
                sb = jnp.where(cc <= rr, sb, NEG_BIG)
            m_old = m_sc[hh, rows]
            m_col = jnp.maximum(m_old[:, 0:1], jnp.max(sb, axis=1, keepdims=True))
            m_new = jnp.broadcast_to(m_col, (FOX_ROW_BLOCK, LANES))
            p = jnp.exp2(sb - m_col).astype(BF16)
            acc_sc[hh, rows] = jnp.exp2(m_old - m_new) * acc_sc[hh, rows] + _dot(p, vbuf[slot, hh])
            m_sc[hh, rows] = m_new

    def step(causal, slot):
        s_next = scores(0, slot)
        for hh in heads:
            s_cur = s_next
            if hh + 1 < FOX_HEADS:
                s_next = scores(hh + 1, slot)
            softmax_pv(hh, s_cur, causal, slot)

    def below_diagonal(ki, carry):
        slot = ki % 2
        start(ki + 1, 1 - slot)
        wait(ki, slot)
        step(False, slot)
        return carry

    lax.fori_loop(0, qi, below_diagonal, 0)

    slot = qi % 2
    wait(qi, slot)
    step(True, slot)
    lane = lax.broadcasted_iota(jnp.int32, (TILE, LANES), 1)
    lo = lane < HEAD_DIM
    normed = []
    for hh in heads:
        acc = acc_sc[hh]
        o = acc / acc[:, V_ONES_LANE:V_ONES_LANE + 1]
        ms = jnp.sum(jnp.where(lo, o * o, 0.0), axis=1, keepdims=True) * (1.0 / HEAD_DIM)
        normed.append(o * lax.rsqrt(ms + RMS_EPS))
    for pr in range(FOX_HEADS // 2):
        o = jnp.where(lo, normed[2 * pr], pltpu.roll(normed[2 * pr + 1], HEAD_DIM, 1))
        sl = slice(pr * LANES, (pr + 1) * LANES)
        o_ref[0, :, sl] = (o * g_ref[:, sl]).astype(BF16)


def _fox_attention(qa, ka, va, g):
    bsz, _, lp, _ = qa.shape
    nt = lp // TILE
    slab = (FOX_HEADS, TILE, LANES)
    return pl.pallas_call(
        _fox_body,
        out_shape=jax.ShapeDtypeStruct((bsz, lp, FOX_WIDTH), BF16),
        grid=(bsz, nt),
        in_specs=[
            pl.BlockSpec((1,) + slab, lambda b, qi: (b, 0, qi, 0)),
            pl.BlockSpec(memory_space=pl.ANY),
            pl.BlockSpec(memory_space=pl.ANY),
            _const_spec((1, FOX_WIDTH)),
        ],
        out_specs=pl.BlockSpec((1, TILE, FOX_WIDTH), lambda b, qi: (b, qi, 0)),
        scratch_shapes=[pltpu.VMEM(slab, F32), pltpu.VMEM(slab, F32),
                        pltpu.VMEM((2,) + slab, BF16), pltpu.VMEM((2,) + slab, BF16),
                        pltpu.SemaphoreType.DMA((2,)), pltpu.SemaphoreType.DMA((2,))],
        compiler_params=_params(("parallel", "parallel")),
        name="fox_attention",
    )(qa, ka, va, g)


def _interleave(*gens):
    gens = list(gens)
    while gens:
        for gen in list(gens):
            try:
                next(gen)
            except StopIteration:
                gens.remove(gen)


def _rwkv_body(z_ref, zp_ref, mu_ref, w0_ref, w2_ref, a0_ref, a2_ref, g2_ref, kk_ref, ka_ref, rk_ref,
               lg_ref, lb_ref, ones_ref, y_ref,
               ar_s, bt_s, kt_s, v_s, bkb_s, gam_s, s_s, y_s, bon_s, gate_s, low_s, arb_s, pv_s):
    i = pl.program_id(1)
    nch = TILE // CHUNK
    w = RWKV_WIDTH
    gw = MXU_DIM
    ngrp = w // gw

    @pl.when(i == 0)
    def _reset():
        s_s[...] = jnp.zeros(s_s.shape, F32)

    z = z_ref[0].astype(F32)
    prev = zp_ref[0][15:16, :].astype(F32)
    prev = jnp.where(i == 0, 0.0, prev)
    row = lax.broadcasted_iota(jnp.int32, (TILE, 1), 0)
    z_shift = jnp.where(row == 0, prev, pltpu.roll(z, 1, 0))
    zs = z + (z_shift - z) * mu_ref[...]
    r = zs[:, 0:w]
    kr = zs[:, w:2 * w]
    vr = zs[:, 2 * w:3 * w]
    dw = zs[:, 3 * w:3 * w + LANES]
    da = zs[:, 3 * w + LANES:3 * w + 2 * LANES]
    dg = zs[:, 3 * w + 2 * LANES:3 * w + 3 * LANES]
    lw = -math.exp(-0.5) * jax.nn.sigmoid(w0_ref[...] + _dot(jnp.tanh(dw).astype(BF16), w2_ref[...]))
    alpha = jax.nn.sigmoid(a0_ref[...] + _dot(da.astype(BF16), a2_ref[...]))
    gate_s[...] = _dot(jax.nn.sigmoid(dg).astype(BF16), g2_ref[...])
    ones = ones_ref[...]
    kk0 = kr * kk_ref[...]
    kk = kk0 * lax.rsqrt(jnp.maximum(_dot((kk0 * kk0).astype(BF16), ones), 1e-24))
    kmod = kr * (1.0 + (alpha - 1.0) * ka_ref[...])
    bon_s[...] = _dot((r * kmod * rk_ref[...]).astype(BF16), ones) * vr

    gcum = _row_cumsum(lw, CHUNK, row)
    glast = jnp.concatenate(
        [jnp.broadcast_to(gcum[c * CHUNK + CHUNK - 1:(c + 1) * CHUNK, :], (CHUNK, w)) for c in range(nch)], axis=0)
    e_out = jnp.exp(-gcum)
    e_tail = jnp.exp(glast - gcum)
    a_t = ((-kk) * jnp.exp(gcum - lw)).astype(BF16)
    r_t = (r * jnp.exp(gcum)).astype(BF16)
    bt_s[...] = (kk * alpha * e_out).astype(BF16)
    kt_s[...] = (kmod * e_out).astype(BF16)
    v_s[...] = vr.astype(BF16)
    b_bar = (kk * alpha * e_tail).astype(BF16)
    k_bar = (kmod * e_tail).astype(BF16)
    gam = jnp.exp(glast)
    for c in range(nch):
        rows = slice(c * CHUNK, (c + 1) * CHUNK)
        ar_s[c, 0:CHUNK, :] = a_t[rows]
        ar_s[c, CHUNK:2 * CHUNK, :] = r_t[rows]
        bkb_s[c, 0:CHUNK, :] = b_bar[rows]
        bkb_s[c, CHUNK:2 * CHUNK, :] = k_bar[rows]
        gam_s[c] = gam[c * CHUNK:c * CHUNK + 1]

    rb = lax.broadcasted_iota(jnp.int32, (gw, gw), 0)
    cb = lax.broadcasted_iota(jnp.int32, (gw, gw), 1)
    same_head = (rb // HEAD_DIM) == (cb // HEAD_DIM)
    r2 = lax.broadcasted_iota(jnp.int32, (2 * CHUNK, gw), 0)
    c2 = lax.broadcasted_iota(jnp.int32, (2 * CHUNK, gw), 1) % CHUNK
    tri2 = c2 < jnp.where(r2 < CHUNK, r2, r2 - CHUNK + 1)
    r1 = lax.broadcasted_iota(jnp.int32, (CHUNK, gw), 0)
    c1 = lax.broadcasted_iota(jnp.int32, (CHUNK, gw), 1) % CHUNK
    level_masks = []
    s = 1
    while s < CHUNK:
        level_masks.append(jnp.logical_and(((r1 ^ c1) & (-s)) == s, r1 > c1))
        s *= 2

    def block_diag(x):
        xb = x.astype(BF16)
        return jnp.where(same_head, jnp.concatenate([xb] * (gw // CHUNK), axis=0), jnp.zeros((), BF16))

    def lanes(g):
        return slice(g * gw, (g + 1) * gw)

    def intra_chunk(chunks):
        keys = [(c, g) for c in chunks for g in range(ngrp)]
        n_ab, low = {}, {}
        for c, g in keys:
            rows = slice(c * CHUNK, (c + 1) * CHUNK)
            ar = ar_s[c, :, lanes(g)]
            pb = jnp.where(tri2, _dot_nt(ar, block_diag(bt_s[rows, lanes(g)])), 0.0)
            pk = jnp.where(tri2, _dot_nt(ar, block_diag(kt_s[rows, lanes(g)])), 0.0)
            n_ab[c, g] = pb[0:CHUNK]
            arb_s[c, :, lanes(g)] = pb[CHUNK:2 * CHUNK].astype(BF16)
            pv_s[c, :, lanes(g)] = _dot(pk.astype(BF16), block_diag(v_s[rows, lanes(g)]))
            low[c, g] = jnp.where(level_masks[0], n_ab[c, g], 0.0)
        yield
        for m in level_masks[1:]:
            x = {}
            for key in keys:
                n_off = jnp.where(m, n_ab[key], 0.0)
                x[key] = n_off + _dot(n_off.astype(BF16), block_diag(low[key]))
            yield
            for key in keys:
                low[key] = low[key] + x[key] + _dot(low[key].astype(BF16), block_diag(x[key]))
            yield
        for c, g in keys:
            low_s[c, :, lanes(g)] = low[c, g].astype(BF16)

    def inter_chunk(chunks):
        for c in chunks:
            rows = slice(c * CHUNK, (c + 1) * CHUNK)
            st, ars, u = {}, {}, {}
            for g in range(ngrp):
                st[g] = s_s[g]
                ars[g] = _dot_nt(ar_s[c, :, lanes(g)], st[g].astype(BF16))
            yield
            for g in range(ngrp):
                wm = ars[g][0:CHUNK] + pv_s[c, 0:CHUNK, lanes(g)]
                u[g] = wm + _dot(low_s[c, :, lanes(g)], block_diag(wm))
            yield
            for g in range(ngrp):
                y_s[rows, lanes(g)] = (ars[g][CHUNK:2 * CHUNK] + pv_s[c, CHUNK:2 * CHUNK, lanes(g)]
                                       + _dot(arb_s[c, :, lanes(g)], block_diag(u[g])))
                uv = jnp.concatenate([u[g].astype(BF16), v_s[rows, lanes(g)]], axis=0)
                upd = _dot_tn(uv, bkb_s[c, :, lanes(g)])
                s_s[g] = st[g] * gam_s[c][:, lanes(g)] + jnp.where(same_head, upd, 0.0)
            yield

    half = nch // 2
    _interleave(intra_chunk(range(0, half)))
    _interleave(intra_chunk(range(half, nch)), inter_chunk(range(0, half)))
    _interleave(inter_chunk(range(half, nch)))

    y = y_s[...]
    mean = _dot(y.astype(BF16), ones) * (1.0 / HEAD_DIM)
    d = y - mean
    var = _dot((d * d).astype(BF16), ones) * (1.0 / HEAD_DIM)
    yn = d * lax.rsqrt(var + GN_EPS)
    y_ref[0] = ((yn * lg_ref[...] + lb_ref[...] + bon_s[...]) * gate_s[...]).astype(BF16)


def _rwkv(zr, mu, w0, w2, a0, a2, g2, k_k, k_a, r_k, lnx_g, lnx_b, ones_bd):
    bsz, lp, rc = zr.shape
    nt = lp // TILE
    nch = TILE // CHUNK
    w = RWKV_WIDTH
    vec = _const_spec((1, w))
    return pl.pallas_call(
        _rwkv_body,
        out_shape=jax.ShapeDtypeStruct((bsz, lp, w), BF16),
        grid=(bsz, nt),
        in_specs=[
            pl.BlockSpec((1, TILE, rc), lambda b, i: (b, i, 0)),
            pl.BlockSpec((1, 16, rc), lambda b, i: (b, jnp.maximum(i * (TILE // 16) - 1, 0), 0)),
            _const_spec((1, rc)),
            vec, _const_spec(w2.shape), vec, _const_spec(a2.shape), _const_spec(g2.shape),
            vec, vec, vec, vec, vec,
            _const_spec(ones_bd.shape),
        ],
        out_specs=pl.BlockSpec((1, TILE, w), lambda b, i: (b, i, 0)),
        scratch_shapes=[
            pltpu.VMEM((nch, 2 * CHUNK, w), BF16),
            pltpu.VMEM((TILE, w), BF16),
            pltpu.VMEM((TILE, w), BF16),
            pltpu.VMEM((TILE, w), BF16),
            pltpu.VMEM((nch, 2 * CHUNK, w), BF16),
            pltpu.VMEM((nch, 1, w), F32),
            pltpu.VMEM((w // MXU_DIM, MXU_DIM, MXU_DIM), F32),
            pltpu.VMEM((TILE, w), F32),
            pltpu.VMEM((TILE, w), F32),
            pltpu.VMEM((TILE, w), F32),
            pltpu.VMEM((nch, CHUNK, w), BF16),
            pltpu.VMEM((nch, CHUNK, w), BF16),
            pltpu.VMEM((nch, 2 * CHUNK, w), F32),
        ],
        compiler_params=_params(("parallel", "arbitrary")),
        name="rwkv7_mix",
    )(zr, zr, mu, w0, w2, a0, a2, g2, k_k, k_a, r_k, lnx_g, lnx_b, ones_bd)


def _outproj_body(x_ref, of_ref, yr_ref, g0_ref, b0_ref, wo1_ref, wo2_ref, g1_ref, b1_ref, wrh_ref, wrl_ref, br_ref,
                  h1_ref, e_ref, gt_ref, rk_ref, cnt_ref, carry_s, *, alpha):
    first = jnp.logical_and(pl.program_id(0) == 0, pl.program_id(1) == 0)

    @pl.when(first)
    def _reset():
        carry_s[...] = jnp.zeros(carry_s.shape, F32)

    h0 = _layer_norm(x_ref[0], g0_ref[...], b0_ref[...])
    mix = _dot(of_ref[0], wo1_ref[...]) + _dot(yr_ref[0], wo2_ref[...])
    h1 = _layer_norm(alpha * h0 + mix, g1_ref[...], b1_ref[...])
    h1_ref[...] = h1
    h_hi = h1.astype(BF16)
    h_lo = (h1 - h_hi.astype(F32)).astype(BF16)
    logits = ((_dot(h_hi, wrh_ref[...]) + _dot(h_lo, wrl_ref[...]))
              + (_dot(h_hi, wrl_ref[...]) + _dot(h_lo, wrh_ref[...]))) + br_ref[...]
    lane = lax.broadcasted_iota(jnp.int32, (TILE, LANES), 1)
    lane_f = lane.astype(F32)
    cur = logits
    vals, idxs = [], []
    for _ in range(TOP_K):
        m = jnp.max(cur, axis=1, keepdims=True)
        idx = jnp.min(jnp.where(cur == m, lane_f, float(LANES)), axis=1, keepdims=True).astype(jnp.int32)
        vals.append(m)
        idxs.append(idx)
        cur = jnp.where(lane == idx, -jnp.inf, cur)
    exps = [jnp.exp(v - vals[0]) for v in vals]
    denom = exps[0] + exps[1] + exps[2] + exps[3]
    hot = [lane == idx for idx in idxs]
    multi = jnp.zeros((TILE, LANES), F32)
    for hk in hot:
        multi = multi + hk.astype(F32)
    rr = lax.broadcasted_iota(jnp.int32, (TILE, TILE), 0)
    cc = lax.broadcasted_iota(jnp.int32, (TILE, TILE), 1)
    before = _dot((cc < rr).astype(BF16), multi.astype(BF16)) + carry_s[...]
    e_out = jnp.zeros((TILE, LANES), jnp.int32)
    g_out = jnp.zeros((TILE, LANES), F32)
    r_out = jnp.zeros((TILE, LANES), F32)
    for kk in range(TOP_K):
        rank = jnp.sum(jnp.where(hot[kk], before, 0.0), axis=1, keepdims=True)
        e_out = jnp.where(lane == kk, idxs[kk], e_out)
        g_out = jnp.where(lane == kk, exps[kk] / denom, g_out)
        r_out = jnp.where(lane == kk, rank, r_out)
    e_ref[...] = e_out
    gt_ref[...] = g_out
    rk_ref[...] = r_out.astype(jnp.int32)
    carry_s[...] = carry_s[...] + jnp.sum(multi, axis=0, keepdims=True)
    cnt_ref[...] = carry_s[...]


def _outproj_router(x, o_fox, y_rwkv, g0, b0, wo1, wo2, g1, b1, w_router_hi, w_router_lo, b_router, alpha):
    bsz, seq, d = x.shape
    nt = seq // TILE
    n = bsz * seq
    row_blk = lambda c: pl.BlockSpec((TILE, c), lambda b, i: (b * nt + i, 0))
    mix_spec = pl.BlockSpec((1, TILE, FOX_WIDTH), lambda b, i: (b, i + 1, 0))
    return pl.pallas_call(
        functools.partial(_outproj_body, alpha=alpha),
        out_shape=(jax.ShapeDtypeStruct((n, d), F32),
                   jax.ShapeDtypeStruct((n, LANES), jnp.int32),
                   jax.ShapeDtypeStruct((n, LANES), F32),
                   jax.ShapeDtypeStruct((n, LANES), jnp.int32),
                   jax.ShapeDtypeStruct((1, LANES), F32)),
        grid=(bsz, nt),
        in_specs=[
            pl.BlockSpec((1, TILE, d), lambda b, i: (b, i, 0)),
            mix_spec, mix_spec,
            _const_spec((1, d)), _const_spec((1, d)),
            _const_spec(wo1.shape), _const_spec(wo2.shape),
            _const_spec((1, d)), _const_spec((1, d)),
            _const_spec(w_router_hi.shape), _const_spec(w_router_lo.shape), _const_spec(b_router.shape),
        ],
        out_specs=(row_blk(d), row_blk(LANES), row_blk(LANES), row_blk(LANES), _const_spec((1, LANES))),
        scratch_shapes=[pltpu.VMEM((1, LANES), F32)],
        compiler_params=_params(("arbitrary", "arbitrary")),
        name="outproj_ln1_router",
    )(x, o_fox, y_rwkv, g0, b0, wo1, wo2, g1, b1, w_router_hi, w_router_lo, b_router)


def _dispatch_body(dest_ref, zstart_ref, nused_ref, h1_ref, xb_hbm, zero_buf, sem, zsem, *, n_experts, n_blocks):
    i = pl.program_id(0)

    def zero_copy(start):
        start = pl.multiple_of(start, MOE_BLOCK)
        return pltpu.make_async_copy(zero_buf, xb_hbm.at[pl.ds(start, MOE_BLOCK)], zsem)

    @pl.when(i == 0)
    def _clear():
        zero_buf[...] = jnp.zeros(zero_buf.shape, F32)
        for e in range(n_experts):
            zero_copy(zstart_ref[e]).start()

        def start_tail(j, carry):
            zero_copy(j * MOE_BLOCK).start()
            return carry

        lax.fori_loop(nused_ref[0], n_blocks, start_tail, 0)
        for e in range(n_experts):
            zero_copy(zstart_ref[e]).wait()

        def wait_tail(j, carry):
            zero_copy(j * MOE_BLOCK).wait()
            return carry

        lax.fori_loop(nused_ref[0], n_blocks, wait_tail, 0)

    def row_copy(t, kk):
        return pltpu.make_async_copy(h1_ref.at[pl.ds(t, 1)],
                                     xb_hbm.at[pl.ds(dest_ref[0, 0, t * TOP_K + kk], 1)], sem)

    def issue(t, carry):
        for kk in range(TOP_K):
            row_copy(t, kk).start(priority=kk % 2)
        return carry

    lax.fori_loop(0, DISPATCH_TOKENS, issue, 0, unroll=DMA_UNROLL)

    def drain(t, carry):
        for kk in range(TOP_K):
            row_copy(t, kk).wait()
        return carry

    lax.fori_loop(0, DISPATCH_TOKENS, drain, 0, unroll=DMA_UNROLL)


def _dispatch(h1, dest, zstart, n_used, cap):
    n, d = h1.shape
    steps = n // DISPATCH_TOKENS
    dest3 = dest.reshape(steps, 1, DISPATCH_TOKENS * TOP_K)
    return pl.pallas_call(
        functools.partial(_dispatch_body, n_experts=zstart.shape[0], n_blocks=cap // MOE_BLOCK),
        out_shape=jax.ShapeDtypeStruct((cap, d), F32),
        grid=(steps,),
        in_specs=[
            pl.BlockSpec((1, 1, DISPATCH_TOKENS * TOP_K), lambda i: (i, 0, 0), memory_space=pltpu.SMEM),
            pl.BlockSpec(memory_space=pltpu.SMEM),
            pl.BlockSpec(memory_space=pltpu.SMEM),
            pl.BlockSpec((DISPATCH_TOKENS, d), lambda i: (i, 0)),
        ],
        out_specs=pl.BlockSpec(memory_space=pl.ANY),
        scratch_shapes=[pltpu.VMEM((MOE_BLOCK, d), F32), pltpu.SemaphoreType.DMA(()), pltpu.SemaphoreType.DMA(())],
        compiler_params=_params(("arbitrary",)),
        name="moe_dispatch",
    )(dest3, zstart, n_used, h1)


def _moe_body(first_ref, count_ref, nused_ref, xb_hbm, wgu_ref, bgu_ref, wd_ref, bd_ref, yb_hbm,
              wgu_bf, wd_bf, xbuf, ybuf, xsem, ysem, *, n_blocks):
    e = pl.program_id(0)
    first = first_ref[e]
    nb = count_ref[e]

    def block_rows(b):
        return pl.ds(pl.multiple_of((first + b) * MOE_BLOCK, MOE_BLOCK), MOE_BLOCK)

    def x_copy(b, s):
        return pltpu.make_async_copy(xb_hbm.at[block_rows(b)], xbuf.at[s], xsem.at[s])

    def y_copy(b, s):
        return pltpu.make_async_copy(ybuf.at[s], yb_hbm.at[block_rows(b)], ysem.at[s])

    @pl.when(nb > 0)
    def _first_rows():
        x_copy(0, 0).start()

    wgu_bf[...] = wgu_ref[0].astype(BF16)
    wd_bf[...] = wd_ref[0].astype(BF16)
    de = wd_bf.shape[0]

    def body(b, carry):
        s = b % 2

        @pl.when(b + 1 < nb)
        def _prefetch():
            x_copy(b + 1, 1 - s).start()

        x_copy(b, s).wait()

        @pl.when(b >= 2)
        def _reuse():
            y_copy(b - 2, s).wait()

        gu = _dot(xbuf[s].astype(BF16), wgu_bf[...]) + bgu_ref[0]
        gate = jnp.minimum(gu[:, :de], SWIGLU_LIMIT)
        up = jnp.clip(gu[:, de:], -SWIGLU_LIMIT, SWIGLU_LIMIT)
        act = (up + 1.0) * (gate * jax.nn.sigmoid(gate * SWIGLU_ALPHA))
        ybuf[s] = _dot(act.astype(BF16), wd_bf[...]) + bd_ref[0]
        y_copy(b, s).start()
        return carry

    lax.fori_loop(0, nb, body, 0)

    @pl.when(nb >= 2)
    def _drain_older():
        y_copy(nb - 2, nb % 2).wait()

    @pl.when(nb >= 1)
    def _drain_last():
        y_copy(nb - 1, (nb - 1) % 2).wait()

    @pl.when(e == pl.num_programs(0) - 1)
    def _clear_tail():
        ybuf[0] = jnp.zeros(ybuf.shape[1:], F32)

        def tail_copy(j):
            rows = pl.ds(pl.multiple_of(j * MOE_BLOCK, MOE_BLOCK), MOE_BLOCK)
            return pltpu.make_async_copy(ybuf.at[0], yb_hbm.at[rows], ysem.at[0])

        def start_tail(j, carry):
            tail_copy(j).start()
            return carry

        def wait_tail(j, carry):
            tail_copy(j).wait()
            return carry

        lax.fori_loop(nused_ref[0], n_blocks, start_tail, 0)
        lax.fori_loop(nused_ref[0], n_blocks, wait_tail, 0)


def _moe_experts(xb, first_block, block_count, n_used, w_gu, b_gu, w_down, b_down):
    cap, d = xb.shape
    ne, _, d2 = w_gu.shape
    de = w_down.shape[1]
    wsel = lambda e, *_: (e, 0, 0)
    grid_spec = pltpu.PrefetchScalarGridSpec(
        num_scalar_prefetch=3,
        grid=(ne,),
        in_specs=[
            pl.BlockSpec(memory_space=pl.ANY),
            pl.BlockSpec((1, d, d2), wsel),
            pl.BlockSpec((1, 1, d2), wsel),
            pl.BlockSpec((1, de, d), wsel),
            pl.BlockSpec((1, 1, d), wsel),
        ],
        out_specs=pl.BlockSpec(memory_space=pl.ANY),
        scratch_shapes=[pltpu.VMEM((d, d2), BF16), pltpu.VMEM((de, d), BF16),
                        pltpu.VMEM((2, MOE_BLOCK, d), F32), pltpu.VMEM((2, MOE_BLOCK, d), F32),
                        pltpu.SemaphoreType.DMA((2,)), pltpu.SemaphoreType.DMA((2,))],
    )
    return pl.pallas_call(
        functools.partial(_moe_body, n_blocks=cap // MOE_BLOCK),
        out_shape=jax.ShapeDtypeStruct((cap, d), F32),
        grid_spec=grid_spec,
        compiler_params=_params(("arbitrary",)),
        name="moe_experts",
    )(first_block, block_count, n_used, xb, w_gu, b_gu.reshape(ne, 1, d2), w_down, b_down.reshape(ne, 1, d))


def _combine_body(dest_ref, dest_next_ref, gt_ref, h1_ref, yb_hbm, g2_ref, b2_ref, o_ref, ybuf, sems, *, alpha):
    i = pl.program_id(0)
    slot = i % 2

    def row_copy(dref, s, t, kk):
        return pltpu.make_async_copy(yb_hbm.at[pl.ds(dref[0, 0, t * TOP_K + kk], 1)],
                                     ybuf.at[s, kk, pl.ds(t, 1)], sems.at[s])

    def issue(dref, s):
        def body(t, carry):
            for kk in range(TOP_K):
                row_copy(dref, s, t, kk).start(priority=kk % 2)
            return carry

        lax.fori_loop(0, COMBINE_TOKENS, body, 0, unroll=DMA_UNROLL)

    @pl.when(i == 0)
    def _prologue():
        issue(dest_ref, 0)

    @pl.when(i + 1 < pl.num_programs(0))
    def _prefetch():
        issue(dest_next_ref, 1 - slot)

    def drain(t, carry):
        for kk in range(TOP_K):
            row_copy(dest_ref, slot, t, kk).wait()
        return carry

    lax.fori_loop(0, COMBINE_TOKENS, drain, 0, unroll=DMA_UNROLL)

    gates = gt_ref[...]
    lane = lax.broadcasted_iota(jnp.int32, gates.shape, 1)
    ff = jnp.zeros(h1_ref.shape, F32)
    for kk in range(TOP_K):
        gk = jnp.sum(jnp.where(lane == kk, gates, 0.0), axis=1, keepdims=True)
        ff = ff + gk * ybuf[slot, kk]
    o_ref[0] = _layer_norm(alpha * h1_ref[...] + ff, g2_ref[...], b2_ref[...])


def _combine(h1, yb, dest, gates, g2, b2, bsz, seq, alpha):
    n, d = h1.shape
    steps = n // COMBINE_TOKENS
    per_b = seq // COMBINE_TOKENS
    dest3 = dest.reshape(steps, 1, COMBINE_TOKENS * TOP_K)
    dest_blk = (1, 1, COMBINE_TOKENS * TOP_K)
    return pl.pallas_call(
        functools.partial(_combine_body, alpha=alpha),
        out_shape=jax.ShapeDtypeStruct((bsz, seq, d), F32),
        grid=(steps,),
        in_specs=[
            pl.BlockSpec(dest_blk, lambda i: (i, 0, 0), memory_space=pltpu.SMEM),
            pl.BlockSpec(dest_blk, lambda i: (jnp.minimum(i + 1, steps - 1), 0, 0), memory_space=pltpu.SMEM),
            pl.BlockSpec((COMBINE_TOKENS, LANES), lambda i: (i, 0)),
            pl.BlockSpec((COMBINE_TOKENS, d), lambda i: (i, 0)),
            pl.BlockSpec(memory_space=pl.ANY),
            _const_spec((1, d)), _const_spec((1, d)),
        ],
        out_specs=pl.BlockSpec((1, COMBINE_TOKENS, d), lambda i: (i // per_b, i % per_b, 0)),
        scratch_shapes=[pltpu.VMEM((2, TOP_K, COMBINE_TOKENS, d), F32), pltpu.SemaphoreType.DMA((2,))],
        compiler_params=_params(("arbitrary",)),
        name="moe_combine_ln2",
    )(dest3, dest3, gates, h1, yb, g2, b2)


def _pad_cols(a, width):
    return jnp.pad(a, ((0, 0), (0, width - a.shape[1])))


def kernel(x, meta, ln0_g, ln0_b, w_in, b_fgate, fox_norm_g, rwkv_mu, w0, w2, a0, a2, g2, k_k, k_a, r_k, lnx_g, lnx_b, w_out, ln1_g, ln1_b, w_router, b_router, w_gu, b_gu, w_down, b_down, ln2_g, ln2_b):
    bsz, seq, d = x.shape
    depth = w_in.shape[0]
    assert depth == 1, "meta-token rows are dropped before the MoE, which is only valid for one layer"
    assert seq % TILE == 0 and seq % COMBINE_TOKENS == 0 and seq % DISPATCH_TOKENS == 0
    alpha = float((2 * depth) ** 0.25)
    n = bsz * seq
    ne = w_gu.shape[1]
    row = lambda a: a.reshape(1, -1)

    w = w_in[0]
    f3 = 3 * FOX_WIDTH
    r3 = 3 * RWKV_WIDTH
    wqkv = w[:, :f3].astype(BF16)
    wf = _pad_cols(w[:, f3:f3 + FOX_HEADS], LANES).astype(BF16)
    bf = _pad_cols(row(b_fgate[0]), LANES)
    wr_raw = w[:, f3 + FOX_HEADS:]

    def regroup(a):
        return jnp.concatenate([
            a[:, :r3],
            _pad_cols(a[:, r3:r3 + DECAY_LORA], LANES),
            _pad_cols(a[:, r3 + DECAY_LORA:r3 + DECAY_LORA + AAA_LORA], LANES),
            a[:, r3 + DECAY_LORA + AAA_LORA:],
        ], axis=1)

    wr = regroup(wr_raw).astype(BF16)
    mu = regroup(row(rwkv_mu[0]))
    w2p = jnp.pad(w2[0], ((0, LANES - DECAY_LORA), (0, 0))).astype(BF16)
    a2p = jnp.pad(a2[0], ((0, LANES - AAA_LORA), (0, 0))).astype(BF16)
    g2b = g2[0].astype(BF16)
    hid = jnp.arange(RWKV_WIDTH) // HEAD_DIM
    ones_bd = (hid[:, None] == hid[None, :]).astype(BF16)
    meta_tile = jnp.pad(meta.astype(x.dtype), ((TILE - N_META, 0), (0, 0)))

    qa, ka, va, zr = _inproj(x, meta_tile, row(ln0_g), row(ln0_b), wqkv, wf, bf, wr)
    o_fox = _fox_attention(qa, ka, va, row(fox_norm_g[0]))
    y_rwkv = _rwkv(zr, mu, row(w0[0]), w2p, row(a0[0]), a2p, g2b, row(k_k[0]), row(k_a[0]),
                   row(r_k[0]), row(lnx_g[0]), row(lnx_b[0]), ones_bd)

    wo = w_out[0].astype(BF16)
    w_rt = _pad_cols(w_router[0], LANES)
    w_rt_hi = w_rt.astype(BF16)
    w_rt_lo = (w_rt - w_rt_hi.astype(F32)).astype(BF16)
    b_rt = jnp.concatenate([row(b_router[0]), jnp.full((1, LANES - ne), NEG_BIG, F32)], axis=1)
    h1, e_pad, gates, rank_pad, counts_f = _outproj_router(
        x, o_fox, y_rwkv, row(ln0_g), row(ln0_b), wo[:FOX_WIDTH], wo[FOX_WIDTH:],
        row(ln1_g[0]), row(ln1_b[0]), w_rt_hi, w_rt_lo, b_rt, alpha)

    counts = counts_f[0, :ne].astype(jnp.int32)
    pcounts = (counts + MOE_BLOCK - 1) // MOE_BLOCK * MOE_BLOCK
    pend = jnp.cumsum(pcounts)
    pstart = pend - pcounts
    nblk = n * TOP_K // MOE_BLOCK + ne
    cap = nblk * MOE_BLOCK
    e_sel = e_pad[:, :TOP_K]
    onehot = (e_sel[:, :, None] == jnp.arange(ne, dtype=jnp.int32)[None, None, :]).astype(jnp.int32)
    dest = (jnp.sum(onehot * pstart[None, None, :], axis=-1) + rank_pad[:, :TOP_K]).reshape(-1)
    n_used = (pend[-1:] // MOE_BLOCK).astype(jnp.int32)
    zstart = jnp.maximum(pend - MOE_BLOCK, 0).astype(jnp.int32)
    first_block = (pstart // MOE_BLOCK).astype(jnp.int32)
    block_count = (pcounts // MOE_BLOCK).astype(jnp.int32)

    xb = _dispatch(h1, dest, zstart, n_used, cap)
    yb = _moe_experts(xb, first_block, block_count, n_used, w_gu[0], b_gu[0], w_down[0], b_down[0])
    return _combine(h1, yb, dest, gates, row(ln2_g[0]), row(ln2_b[0]), bsz, seq, alpha)
```

```python
import functools
import math

import jax
import jax.numpy as jnp
from jax import lax
from jax.experimental import pallas as pl
from jax.experimental.pallas import tpu as pltpu

N_META = 16
HEAD_DIM = 64
FOX_HEADS = 8
RWKV_HEADS = 8
FOX_WIDTH = FOX_HEADS * HEAD_DIM
RWKV_WIDTH = RWKV_HEADS * HEAD_DIM
DECAY_LORA = 64
AAA_LORA = 64
GATE_LORA = 128
TOP_K = 4
SWIGLU_LIMIT = 7.0
SWIGLU_ALPHA = 1.702
LN_EPS = 1e-5
GN_EPS = 64e-5
RMS_EPS = 1e-6
NEG_BIG = -1e30
LOG2E = math.log2(math.e)

LANES = 128
MXU_DIM = 256
TILE = 512
CHUNK = 64
FOX_ROW_BLOCK = 256
MOE_BLOCK = 256
MOE_X_DEPTH = 3
DISPATCH_TOKENS = 256
COMBINE_TOKENS = 256
DMA_UNROLL = 8
VMEM_LIMIT = 56 * 1024 * 1024

Q_BIAS_LANE = HEAD_DIM
K_BIAS_LANE = HEAD_DIM + 3
V_ONES_LANE = HEAD_DIM

F32 = jnp.float32
BF16 = jnp.bfloat16


def _dot(a, b):
    return jnp.dot(a, b, preferred_element_type=F32)


def _dot_nt(a, b):
    return lax.dot_general(a, b, (((1,), (1,)), ((), ())), preferred_element_type=F32)


def _dot_tn(a, b):
    return lax.dot_general(a, b, (((0,), (0,)), ((), ())), preferred_element_type=F32)


def _layer_norm(x, g, b):
    mu = jnp.mean(x, axis=-1, keepdims=True)
    xc = x - mu
    var = jnp.mean(xc * xc, axis=-1, keepdims=True)
    return xc * lax.rsqrt(var + LN_EPS) * g + b


def _softplus(x):
    return jnp.maximum(x, 0.0) + jnp.log1p(jnp.exp(-jnp.abs(x)))


def _row_cumsum(x, period, row):
    pos = row % period
    sh = 1
    while sh < period:
        x = x + jnp.where(pos >= sh, pltpu.roll(x, sh, 0), 0.0)
        sh *= 2
    return x


def _const_spec(shape):
    nd = len(shape)
    return pl.BlockSpec(shape, lambda *_: (0,) * nd)


def _params(sem):
    return pltpu.CompilerParams(dimension_semantics=sem, vmem_limit_bytes=VMEM_LIMIT)


def _inproj_body(x_ref, meta_ref, g_ref, b_ref, wqkv_ref, wf_ref, bf_ref, wr_ref,
                 qa_ref, ka_ref, va_ref, zr_ref, carry_s):
    i = pl.program_id(1)

    @pl.when(i == 0)
    def _reset():
        carry_s[...] = jnp.zeros(carry_s.shape, F32)

    x = jnp.where(i == 0, meta_ref[...], x_ref[0])
    h = _layer_norm(x, g_ref[...], b_ref[...])
    row = lax.broadcasted_iota(jnp.int32, (TILE, 1), 0)
    valid = jnp.logical_or(i > 0, row >= TILE - N_META)
    hb = jnp.where(valid, h, 0.0).astype(BF16)
    zr_ref[0] = _dot(hb, wr_ref[...]).astype(BF16)

    f = _dot(hb, wf_ref[...]) + bf_ref[...]
    lf = jnp.where(valid, -_softplus(-f), 0.0)
    cs = _row_cumsum(lf, TILE, row) + carry_s[...]
    carry_s[...] = cs[TILE - 1:TILE, :]
    c2 = cs * LOG2E

    qkv = _dot(hb, wqkv_ref[...])
    w = FOX_WIDTH
    qs = qkv[:, :w] * (LOG2E * HEAD_DIM ** -0.5)
    ks = qkv[:, w:2 * w]
    vs = qkv[:, 2 * w:]
    lane = lax.broadcasted_iota(jnp.int32, (TILE, LANES), 1)
    in_head = lane < HEAD_DIM
    for hd in range(FOX_HEADS):
        pair = slice(LANES * (hd // 2), LANES * (hd // 2 + 1))

        def head_lanes(a):
            slab = a[:, pair]
            return slab if hd % 2 == 0 else pltpu.roll(slab, HEAD_DIM, 1)

        col = jnp.broadcast_to(c2[:, hd:hd + 1], (TILE, LANES))
        hi = col.astype(BF16).astype(F32)
        rem = col - hi
        mid = rem.astype(BF16).astype(F32)
        lo = rem - mid
        q_bias = jnp.where(lane == Q_BIAS_LANE, hi, jnp.where(lane == Q_BIAS_LANE + 1, mid, lo))
        q_tail = jnp.where(lane < Q_BIAS_LANE + 3, q_bias, jnp.where(lane < Q_BIAS_LANE + 6, 1.0, 0.0))
        k_bias = jnp.where(lane == K_BIAS_LANE, -hi, jnp.where(lane == K_BIAS_LANE + 1, -mid, -lo))
        k_bias = jnp.where(valid, k_bias, jnp.where(lane == K_BIAS_LANE, NEG_BIG, 0.0))
        k_tail = jnp.where(lane < K_BIAS_LANE, 1.0, jnp.where(lane < K_BIAS_LANE + 3, k_bias, 0.0))
        qa_ref[0, hd] = jnp.where(in_head, head_lanes(qs), q_tail).astype(BF16)
        ka_ref[0, hd] = jnp.where(in_head, head_lanes(ks), k_tail).astype(BF16)
        va_ref[0, hd] = jnp.where(in_head, head_lanes(vs), jnp.where(lane == V_ONES_LANE, 1.0, 0.0)).astype(BF16)


def _inproj(x, meta_tile, g0, b0, wqkv, wf, bf, wr):
    bsz, seq, d = x.shape
    nt = seq // TILE + 1
    lp = nt * TILE
    rc = wr.shape[1]
    slab = jax.ShapeDtypeStruct((bsz, FOX_HEADS, lp, LANES), BF16)
    slab_spec = pl.BlockSpec((1, FOX_HEADS, TILE, LANES), lambda b, i: (b, 0, i, 0))
    return pl.pallas_call(
        _inproj_body,
        out_shape=(slab, slab, slab, jax.ShapeDtypeStruct((bsz, lp, rc), BF16)),
        grid=(bsz, nt),
        in_specs=[
            pl.BlockSpec((1, TILE, d), lambda b, i: (b, jnp.maximum(i - 1, 0), 0)),
            _const_spec((TILE, d)),
            _const_spec((1, d)), _const_spec((1, d)),
            _const_spec(wqkv.shape), _const_spec(wf.shape), _const_spec(bf.shape), _const_spec(wr.shape),
        ],
        out_specs=(slab_spec, slab_spec, slab_spec, pl.BlockSpec((1, TILE, rc), lambda b, i: (b, i, 0))),
        scratch_shapes=[pltpu.VMEM((1, LANES), F32)],
        compiler_params=_params(("parallel", "arbitrary")),
        name="ln0_inproj",
    )(x, meta_tile, g0, b0, wqkv, wf, bf, wr)


def _fox_body(q_ref, k_hbm, v_hbm, g_ref, o_ref, m_sc, acc_sc, kbuf, vbuf, ksem, vsem):
    b = pl.program_id(0)
    qi = pl.program_id(1)

    def kv_copy(ki, slot):
        rows = pl.ds(pl.multiple_of(ki * TILE, TILE), TILE)
        return (pltpu.make_async_copy(k_hbm.at[b, :, rows, :], kbuf.at[slot], ksem.at[slot]),
                pltpu.make_async_copy(v_hbm.at[b, :, rows, :], vbuf.at[slot], vsem.at[slot]))

    def start(ki, slot):
        for cp in kv_copy(ki, slot):
            cp.start()

    def wait(ki, slot):
        for cp in kv_copy(ki, slot):
            cp.wait()

    start(0, 0)
    m_sc[...] = jnp.full(m_sc.shape, NEG_BIG, F32)
    acc_sc[...] = jnp.zeros(acc_sc.shape, F32)

    heads = range(FOX_HEADS)

    def scores(hh, slot):
        return _dot_nt(q_ref[0, hh], kbuf[slot, hh])

    def softmax_pv(hh, s, causal, slot):
        for rb in range(TILE // FOX_ROW_BLOCK):
            rows = slice(rb * FOX_ROW_BLOCK, (rb + 1) * FOX_ROW_BLOCK)
            sb = s[rows]
            if causal:
                rr = lax.broadcasted_iota(jnp.int32, (FOX_ROW_BLOCK, TILE), 0) + rb * FOX_ROW_BLOCK
                cc = lax.broadcasted_iota(jnp.int32, (FOX_ROW_BLOCK, TILE), 1)
                sb = jnp.where(cc <= rr, sb, NEG_BIG)
            m_old = m_sc[hh, rows]
            m_col = jnp.maximum(m_old[:, 0:1], jnp.max(sb, axis=1, keepdims=True))
            m_new = jnp.broadcast_to(m_col, (FOX_ROW_BLOCK, LANES))
            p = jnp.exp2(sb - m_col).astype(BF16)
            acc_sc[hh, rows] = jnp.exp2(m_old - m_new) * acc_sc[hh, rows] + _dot(p, vbuf[slot, hh])
            m_sc[hh, rows] = m_new

    def step(causal, slot):
        s_next = scores(0, slot)
        for hh in heads:
            s_cur = s_next
            if hh + 1 < FOX_HEADS:
                s_next = scores(hh + 1, slot)
            softmax_pv(hh, s_cur, causal, slot)

    def below_diagonal(ki, carry):
        slot = ki % 2
        start(ki + 1, 1 - slot)
        wait(ki, slot)
        step(False, slot)
        return carry

    lax.fori_loop(0, qi, below_diagonal, 0)

    slot = qi % 2
    wait(qi, slot)
    step(True, slot)
    lane = lax.broadcasted_iota(jnp.int32, (TILE, LANES), 1)
    lo = lane < HEAD_DIM
    normed = []
    for hh in heads:
        acc = acc_sc[hh]
        o = acc / acc[:, V_ONES_LANE:V_ONES_LANE + 1]
        ms = jnp.sum(jnp.where(lo, o * o, 0.0), axis=1, keepdims=True) * (1.0 / HEAD_DIM)
        normed.append(o * lax.rsqrt(ms + RMS_EPS))
    for pr in range(FOX_HEADS // 2):
        o = jnp.where(lo, normed[2 * pr], pltpu.roll(normed[2 * pr + 1], HEAD_DIM, 1))
        sl = slice(pr * LANES, (pr + 1) * LANES)
        o_ref[0, :, sl] = (o * g_ref[:, sl]).astype(BF16)


def _fox_attention(qa, ka, va, g):
    bsz, _, lp, _ = qa.shape
    nt = lp // TILE
    slab = (FOX_HEADS, TILE, LANES)
    return pl.pallas_call(
        _fox_body,
        out_shape=jax.ShapeDtypeStruct((bsz, lp, FOX_WIDTH), BF16),
        grid=(bsz, nt),
        in_specs=[
            pl.BlockSpec((1,) + slab, lambda b, qi: (b, 0, qi, 0)),
            pl.BlockSpec(memory_space=pl.ANY),
            pl.BlockSpec(memory_space=pl.ANY),
            _const_spec((1, FOX_WIDTH)),
        ],
        out_specs=pl.BlockSpec((1, TILE, FOX_WIDTH), lambda b, qi: (b, qi, 0)),
        scratch_shapes=[pltpu.VMEM(slab, F32), pltpu.VMEM(slab, F32),
                        pltpu.VMEM((2,) + slab, BF16), pltpu.VMEM((2,) + slab, BF16),
                        pltpu.SemaphoreType.DMA((2,)), pltpu.SemaphoreType.DMA((2,))],
        compiler_params=_params(("parallel", "parallel")),
        name="fox_attention",
    )(qa, ka, va, g)


def _interleave(*gens):
    gens = list(gens)
    while gens:
        for gen in list(gens):
            try:
                next(gen)
            except StopIteration:
                gens.remove(gen)


def _rwkv_body(z_ref, zp_ref, mu_ref, w0_ref, w2_ref, a0_ref, a2_ref, g2_ref, kk_ref, ka_ref, rk_ref,
               lg_ref, lb_ref, ones_ref, y_ref,
               ar_s, bt_s, kt_s, v_s, bkb_s, gam_s, s_s, y_s, bon_s, gate_s, low_s, arb_s, pv_s):
    i = pl.program_id(1)
    nch = TILE // CHUNK
    w = RWKV_WIDTH
    gw = MXU_DIM
    ngrp = w // gw

    @pl.when(i == 0)
    def _reset():
        s_s[...] = jnp.zeros(s_s.shape, F32)

    z = z_ref[0].astype(F32)
    prev = zp_ref[0][15:16, :].astype(F32)
    prev = jnp.where(i == 0, 0.0, prev)
    row = lax.broadcasted_iota(jnp.int32, (TILE, 1), 0)
    z_shift = jnp.where(row == 0, prev, pltpu.roll(z, 1, 0))
    zs = z + (z_shift - z) * mu_ref[...]
    r = zs[:, 0:w]
    kr = zs[:, w:2 * w]
    vr = zs[:, 2 * w:3 * w]
    dw = zs[:, 3 * w:3 * w + LANES]
    da = zs[:, 3 * w + LANES:3 * w + 2 * LANES]
    dg = zs[:, 3 * w + 2 * LANES:3 * w + 3 * LANES]
    lw = -math.exp(-0.5) * jax.nn.sigmoid(w0_ref[...] + _dot(jnp.tanh(dw).astype(BF16), w2_ref[...]))
    alpha = jax.nn.sigmoid(a0_ref[...] + _dot(da.astype(BF16), a2_ref[...]))
    gate_s[...] = _dot(jax.nn.sigmoid(dg).astype(BF16), g2_ref[...])
    ones = ones_ref[...]
    kk0 = kr * kk_ref[...]
    kk = kk0 * lax.rsqrt(jnp.maximum(_dot((kk0 * kk0).astype(BF16), ones), 1e-24))
    kmod = kr * (1.0 + (alpha - 1.0) * ka_ref[...])
    bon_s[...] = _dot((r * kmod * rk_ref[...]).astype(BF16), ones) * vr

    gcum = _row_cumsum(lw, CHUNK, row)
    glast = jnp.concatenate(
        [jnp.broadcast_to(gcum[c * CHUNK + CHUNK - 1:(c + 1) * CHUNK, :], (CHUNK, w)) for c in range(nch)], axis=0)
    e_out = jnp.exp(-gcum)
    e_tail = jnp.exp(glast - gcum)
    a_t = ((-kk) * jnp.exp(gcum - lw)).astype(BF16)
    r_t = (r * jnp.exp(gcum)).astype(BF16)
    bt_s[...] = (kk * alpha * e_out).astype(BF16)
    kt_s[...] = (kmod * e_out).astype(BF16)
    v_s[...] = vr.astype(BF16)
    b_bar = (kk * alpha * e_tail).astype(BF16)
    k_bar = (kmod * e_tail).astype(BF16)
    gam = jnp.exp(glast)
    for c in range(nch):
        rows = slice(c * CHUNK, (c + 1) * CHUNK)
        ar_s[c, 0:CHUNK, :] = a_t[rows]
        ar_s[c, CHUNK:2 * CHUNK, :] = r_t[rows]
        bkb_s[c, 0:CHUNK, :] = b_bar[rows]
        bkb_s[c, CHUNK:2 * CHUNK, :] = k_bar[rows]
        gam_s[c] = gam[c * CHUNK:c * CHUNK + 1]

    rb = lax.broadcasted_iota(jnp.int32, (gw, gw), 0)
    cb = lax.broadcasted_iota(jnp.int32, (gw, gw), 1)
    same_head = (rb // HEAD_DIM) == (cb // HEAD_DIM)
    r2 = lax.broadcasted_iota(jnp.int32, (2 * CHUNK, gw), 0)
    c2 = lax.broadcasted_iota(jnp.int32, (2 * CHUNK, gw), 1) % CHUNK
    tri2 = c2 < jnp.where(r2 < CHUNK, r2, r2 - CHUNK + 1)
    r1 = lax.broadcasted_iota(jnp.int32, (CHUNK, gw), 0)
    c1 = lax.broadcasted_iota(jnp.int32, (CHUNK, gw), 1) % CHUNK
    level_masks = []
    s = 1
    while s < CHUNK:
        level_masks.append(jnp.logical_and(((r1 ^ c1) & (-s)) == s, r1 > c1))
        s *= 2

    def block_diag(x):
        xb = x.astype(BF16)
        return jnp.where(same_head, jnp.concatenate([xb] * (gw // CHUNK), axis=0), jnp.zeros((), BF16))

    def lanes(g):
        return slice(g * gw, (g + 1) * gw)

    def intra_chunk(chunks):
        keys = [(c, g) for c in chunks for g in range(ngrp)]
        n_ab, low = {}, {}
        for c, g in keys:
            rows = slice(c * CHUNK, (c + 1) * CHUNK)
            ar = ar_s[c, :, lanes(g)]
            pb = jnp.where(tri2, _dot_nt(ar, block_diag(bt_s[rows, lanes(g)])), 0.0)
            pk = jnp.where(tri2, _dot_nt(ar, block_diag(kt_s[rows, lanes(g)])), 0.0)
            n_ab[c, g] = pb[0:CHUNK]
            arb_s[c, :, lanes(g)] = pb[CHUNK:2 * CHUNK].astype(BF16)
            pv_s[c, :, lanes(g)] = _dot(pk.astype(BF16), block_diag(v_s[rows, lanes(g)]))
            low[c, g] = jnp.where(level_masks[0], n_ab[c, g], 0.0)
        yield
        for m in level_masks[1:]:
            x = {}
            for key in keys:
                n_off = jnp.where(m, n_ab[key], 0.0)
                x[key] = n_off + _dot(n_off.astype(BF16), block_diag(low[key]))
            yield
            for key in keys:
                low[key] = low[key] + x[key] + _dot(low[key].astype(BF16), block_diag(x[key]))
            yield
        for c, g in keys:
            low_s[c, :, lanes(g)] = low[c, g].astype(BF16)

    def inter_chunk(chunks):
        for c in chunks:
            rows = slice(c * CHUNK, (c + 1) * CHUNK)
            st, ars, u = {}, {}, {}
            for g in range(ngrp):
                st[g] = s_s[g]
                ars[g] = _dot_nt(ar_s[c, :, lanes(g)], st[g].astype(BF16))
            yield
            for g in range(ngrp):
                wm = ars[g][0:CHUNK] + pv_s[c, 0:CHUNK, lanes(g)]
                u[g] = wm + _dot(low_s[c, :, lanes(g)], block_diag(wm))
            yield
            for g in range(ngrp):
                y_s[rows, lanes(g)] = (ars[g][CHUNK:2 * CHUNK] + pv_s[c, CHUNK:2 * CHUNK, lanes(g)]
                                       + _dot(arb_s[c, :, lanes(g)], block_diag(u[g])))
                uv = jnp.concatenate([u[g].astype(BF16), v_s[rows, lanes(g)]], axis=0)
                upd = _dot_tn(uv, bkb_s[c, :, lanes(g)])
                s_s[g] = st[g] * gam_s[c][:, lanes(g)] + jnp.where(same_head, upd, 0.0)
            yield

    half = nch // 2
    _interleave(intra_chunk(range(0, half)))
    _interleave(intra_chunk(range(half, nch)), inter_chunk(range(0, half)))
    _interleave(inter_chunk(range(half, nch)))

    y = y_s[...]
    mean = _dot(y.astype(BF16), ones) * (1.0 / HEAD_DIM)
    d = y - mean
    var = _dot((d * d).astype(BF16), ones) * (1.0 / HEAD_DIM)
    yn = d * lax.rsqrt(var + GN_EPS)
    y_ref[0] = ((yn * lg_ref[...] + lb_ref[...] + bon_s[...]) * gate_s[...]).astype(BF16)


def _rwkv(zr, mu, w0, w2, a0, a2, g2, k_k, k_a, r_k, lnx_g, lnx_b, ones_bd):
    bsz, lp, rc = zr.shape
    nt = lp // TILE
    nch = TILE // CHUNK
    w = RWKV_WIDTH
    vec = _const_spec((1, w))
    return pl.pallas_call(
        _rwkv_body,
        out_shape=jax.ShapeDtypeStruct((bsz, lp, w), BF16),
        grid=(bsz, nt),
        in_specs=[
            pl.BlockSpec((1, TILE, rc), lambda b, i: (b, i, 0)),
            pl.BlockSpec((1, 16, rc), lambda b, i: (b, jnp.maximum(i * (TILE // 16) - 1, 0), 0)),
            _const_spec((1, rc)),
            vec, _const_spec(w2.shape), vec, _const_spec(a2.shape), _const_spec(g2.shape),
            vec, vec, vec, vec, vec,
            _const_spec(ones_bd.shape),
        ],
        out_specs=pl.BlockSpec((1, TILE, w), lambda b, i: (b, i, 0)),
        scratch_shapes=[
            pltpu.VMEM((nch, 2 * CHUNK, w), BF16),
            pltpu.VMEM((TILE, w), BF16),
            pltpu.VMEM((TILE, w), BF16),
            pltpu.VMEM((TILE, w), BF16),
            pltpu.VMEM((nch, 2 * CHUNK, w), BF16),
            pltpu.VMEM((nch, 1, w), F32),
            pltpu.VMEM((w // MXU_DIM, MXU_DIM, MXU_DIM), F32),
            pltpu.VMEM((TILE, w), F32),
            pltpu.VMEM((TILE, w), F32),
            pltpu.VMEM((TILE, w), F32),
            pltpu.VMEM((nch, CHUNK, w), BF16),
            pltpu.VMEM((nch, CHUNK, w), BF16),
            pltpu.VMEM((nch, 2 * CHUNK, w), F32),
        ],
        compiler_params=_params(("parallel", "arbitrary")),
        name="rwkv7_mix",
    )(zr, zr, mu, w0, w2, a0, a2, g2, k_k, k_a, r_k, lnx_g, lnx_b, ones_bd)


def _outproj_body(x_ref, of_ref, yr_ref, g0_ref, b0_ref, wo1_ref, wo2_ref, g1_ref, b1_ref, wrh_ref, wrl_ref, br_ref,
                  h1_ref, e_ref, gt_ref, rk_ref, cnt_ref, carry_s, *, alpha):
    first = jnp.logical_and(pl.program_id(0) == 0, pl.program_id(1) == 0)

    @pl.when(first)
    def _reset():
        carry_s[...] = jnp.zeros(carry_s.shape, F32)

    h0 = _layer_norm(x_ref[0], g0_ref[...], b0_ref[...])
    mix = _dot(of_ref[0], wo1_ref[...]) + _dot(yr_ref[0], wo2_ref[...])
    h1 = _layer_norm(alpha * h0 + mix, g1_ref[...], b1_ref[...])
    h1_ref[...] = h1
    h_hi = h1.astype(BF16)
    h_lo = (h1 - h_hi.astype(F32)).astype(BF16)
    logits = ((_dot(h_hi, wrh_ref[...]) + _dot(h_lo, wrl_ref[...]))
              + (_dot(h_hi, wrl_ref[...]) + _dot(h_lo, wrh_ref[...]))) + br_ref[...]
    lane = lax.broadcasted_iota(jnp.int32, (TILE, LANES), 1)
    lane_f = lane.astype(F32)
    cur = logits
    vals, idxs = [], []
    for _ in range(TOP_K):
        m = jnp.max(cur, axis=1, keepdims=True)
        idx = jnp.min(jnp.where(cur == m, lane_f, float(LANES)), axis=1, keepdims=True).astype(jnp.int32)
        vals.append(m)
        idxs.append(idx)
        cur = jnp.where(lane == idx, -jnp.inf, cur)
    exps = [jnp.exp(v - vals[0]) for v in vals]
    denom = exps[0] + exps[1] + exps[2] + exps[3]
    hot = [lane == idx for idx in idxs]
    multi = jnp.zeros((TILE, LANES), F32)
    for hk in hot:
        multi = multi + hk.astype(F32)
    rr = lax.broadcasted_iota(jnp.int32, (TILE, TILE), 0)
    cc = lax.broadcasted_iota(jnp.int32, (TILE, TILE), 1)
    before = _dot((cc < rr).astype(BF16), multi.astype(BF16)) + carry_s[...]
    e_out = jnp.zeros((TILE, LANES), jnp.int32)
    g_out = jnp.zeros((TILE, LANES), F32)
    r_out = jnp.zeros((TILE, LANES), F32)
    for kk in range(TOP_K):
        rank = jnp.sum(jnp.where(hot[kk], before, 0.0), axis=1, keepdims=True)
        e_out = jnp.where(lane == kk, idxs[kk], e_out)
        g_out = jnp.where(lane == kk, exps[kk] / denom, g_out)
        r_out = jnp.where(lane == kk, rank, r_out)
    e_ref[...] = e_out
    gt_ref[...] = g_out
    rk_ref[...] = r_out.astype(jnp.int32)
    carry_s[...] = carry_s[...] + jnp.sum(multi, axis=0, keepdims=True)
    cnt_ref[...] = carry_s[...]


def _outproj_router(x, o_fox, y_rwkv, g0, b0, wo1, wo2, g1, b1, w_router_hi, w_router_lo, b_router, alpha):
    bsz, seq, d = x.shape
    nt = seq // TILE
    n = bsz * seq
    row_blk = lambda c: pl.BlockSpec((TILE, c), lambda b, i: (b * nt + i, 0))
    mix_spec = pl.BlockSpec((1, TILE, FOX_WIDTH), lambda b, i: (b, i + 1, 0))
    return pl.pallas_call(
        functools.partial(_outproj_body, alpha=alpha),
        out_shape=(jax.ShapeDtypeStruct((n, d), F32),
                   jax.ShapeDtypeStruct((n, LANES), jnp.int32),
                   jax.ShapeDtypeStruct((n, LANES), F32),
                   jax.ShapeDtypeStruct((n, LANES), jnp.int32),
                   jax.ShapeDtypeStruct((1, LANES), F32)),
        grid=(bsz, nt),
        in_specs=[
            pl.BlockSpec((1, TILE, d), lambda b, i: (b, i, 0)),
            mix_spec, mix_spec,
            _const_spec((1, d)), _const_spec((1, d)),
            _const_spec(wo1.shape), _const_spec(wo2.shape),
            _const_spec((1, d)), _const_spec((1, d)),
            _const_spec(w_router_hi.shape), _const_spec(w_router_lo.shape), _const_spec(b_router.shape),
        ],
        out_specs=(row_blk(d), row_blk(LANES), row_blk(LANES), row_blk(LANES), _const_spec((1, LANES))),
        scratch_shapes=[pltpu.VMEM((1, LANES), F32)],
        compiler_params=_params(("arbitrary", "arbitrary")),
        name="outproj_ln1_router",
    )(x, o_fox, y_rwkv, g0, b0, wo1, wo2, g1, b1, w_router_hi, w_router_lo, b_router)


def _dispatch_body(dest_ref, zstart_ref, nused_ref, h1_ref, xb_hbm, zero_buf, sem, zsem, *, n_experts, n_blocks):
    i = pl.program_id(0)

    def zero_copy(start):
        start = pl.multiple_of(start, MOE_BLOCK)
        return pltpu.make_async_copy(zero_buf, xb_hbm.at[pl.ds(start, MOE_BLOCK)], zsem)

    @pl.when(i == 0)
    def _clear():
        zero_buf[...] = jnp.zeros(zero_buf.shape, F32)
        for e in range(n_experts):
            zero_copy(zstart_ref[e]).start()

        def start_tail(j, carry):
            zero_copy(j * MOE_BLOCK).start()
            return carry

        lax.fori_loop(nused_ref[0], n_blocks, start_tail, 0)
        for e in range(n_experts):
            zero_copy(zstart_ref[e]).wait()

        def wait_tail(j, carry):
            zero_copy(j * MOE_BLOCK).wait()
            return carry

        lax.fori_loop(nused_ref[0], n_blocks, wait_tail, 0)

    def row_copy(t, kk):
        return pltpu.make_async_copy(h1_ref.at[pl.ds(t, 1)],
                                     xb_hbm.at[pl.ds(dest_ref[0, 0, t * TOP_K + kk], 1)], sem)

    def issue(t, carry):
        for kk in range(TOP_K):
            row_copy(t, kk).start(priority=kk % 2)
        return carry

    lax.fori_loop(0, DISPATCH_TOKENS, issue, 0, unroll=DMA_UNROLL)

    def drain(t, carry):
        for kk in range(TOP_K):
            row_copy(t, kk).wait()
        return carry

    lax.fori_loop(0, DISPATCH_TOKENS, drain, 0, unroll=DMA_UNROLL)


def _dispatch(h1, dest, zstart, n_used, cap):
    n, d = h1.shape
    steps = n // DISPATCH_TOKENS
    dest3 = dest.reshape(steps, 1, DISPATCH_TOKENS * TOP_K)
    return pl.pallas_call(
        functools.partial(_dispatch_body, n_experts=zstart.shape[0], n_blocks=cap // MOE_BLOCK),
        out_shape=jax.ShapeDtypeStruct((cap, d), F32),
        grid=(steps,),
        in_specs=[
            pl.BlockSpec((1, 1, DISPATCH_TOKENS * TOP_K), lambda i: (i, 0, 0), memory_space=pltpu.SMEM),
            pl.BlockSpec(memory_space=pltpu.SMEM),
            pl.BlockSpec(memory_space=pltpu.SMEM),
            pl.BlockSpec((DISPATCH_TOKENS, d), lambda i: (i, 0)),
        ],
        out_specs=pl.BlockSpec(memory_space=pl.ANY),
        scratch_shapes=[pltpu.VMEM((MOE_BLOCK, d), F32), pltpu.SemaphoreType.DMA(()), pltpu.SemaphoreType.DMA(())],
        compiler_params=_params(("arbitrary",)),
        name="moe_dispatch",
    )(dest3, zstart, n_used, h1)


def _moe_body(first_ref, count_ref, nused_ref, xb_hbm, wgu_ref, bgu_ref, wd_ref, bd_ref, yb_hbm,
              wgu_bf, wd_bf, xbuf, ybuf, xsem, ysem, *, n_blocks):
    e = pl.program_id(0)
    first = first_ref[e]
    nb = count_ref[e]
    nused = nused_ref[0]
    x_depth = xbuf.shape[0]
    y_depth = ybuf.shape[0]

    def block_rows(g):
        return pl.ds(pl.multiple_of(g * MOE_BLOCK, MOE_BLOCK), MOE_BLOCK)

    def x_copy(g):
        s = g % x_depth
        return pltpu.make_async_copy(xb_hbm.at[block_rows(g)], xbuf.at[s], xsem.at[s])

    def y_copy(g):
        s = g % y_depth
        return pltpu.make_async_copy(ybuf.at[s], yb_hbm.at[block_rows(g)], ysem.at[s])

    @pl.when(e == 0)
    def _prime():
        for g0 in range(x_depth - 1):
            @pl.when(g0 < nused)
            def _():
                x_copy(g0).start()

    wgu_bf[...] = wgu_ref[0].astype(BF16)
    wd_bf[...] = wd_ref[0].astype(BF16)
    de = wd_bf.shape[0]

    def body(b, carry):
        g = first + b

        @pl.when(g + x_depth - 1 < nused)
        def _prefetch():
            x_copy(g + x_depth - 1).start()

        x_copy(g).wait()

        @pl.when(g >= y_depth)
        def _reuse():
            y_copy(g - y_depth).wait()

        gu = _dot(xbuf[g % x_depth].astype(BF16), wgu_bf[...]) + bgu_ref[0]
        gate = jnp.minimum(gu[:, :de], SWIGLU_LIMIT)
        up = jnp.clip(gu[:, de:], -SWIGLU_LIMIT, SWIGLU_LIMIT)
        act = (up + 1.0) * (gate * jax.nn.sigmoid(gate * SWIGLU_ALPHA))
        ybuf[g % y_depth] = _dot(act.astype(BF16), wd_bf[...]) + bd_ref[0]
        y_copy(g).start()
        return carry

    lax.fori_loop(0, nb, body, 0)

    @pl.when(e == pl.num_programs(0) - 1)
    def _clear_tail():
        for back in range(y_depth, 0, -1):
            @pl.when(nused >= back)
            def _():
                y_copy(nused - back).wait()

        ybuf[0] = jnp.zeros(ybuf.shape[1:], F32)

        def tail_copy(j):
            rows = pl.ds(pl.multiple_of(j * MOE_BLOCK, MOE_BLOCK), MOE_BLOCK)
            return pltpu.make_async_copy(ybuf.at[0], yb_hbm.at[rows], ysem.at[0])

        def start_tail(j, carry):
            tail_copy(j).start()
            return carry

        def wait_tail(j, carry):
            tail_copy(j).wait()
            return carry

        lax.fori_loop(nused_ref[0], n_blocks, start_tail, 0)
        lax.fori_loop(nused_ref[0], n_blocks, wait_tail, 0)


def _moe_experts(xb, first_block, block_count, n_used, w_gu, b_gu, w_down, b_down):
    cap, d = xb.shape
    ne, _, d2 = w_gu.shape
    de = w_down.shape[1]
    wsel = lambda e, *_: (e, 0, 0)
    grid_spec = pltpu.PrefetchScalarGridSpec(
        num_scalar_prefetch=3,
        grid=(ne,),
        in_specs=[
            pl.BlockSpec(memory_space=pl.ANY),
            pl.BlockSpec((1, d, d2), wsel),
            pl.BlockSpec((1, 1, d2), wsel),
            pl.BlockSpec((1, de, d), wsel),
            pl.BlockSpec((1, 1, d), wsel),
        ],
        out_specs=pl.BlockSpec(memory_space=pl.ANY),
        scratch_shapes=[pltpu.VMEM((d, d2), BF16), pltpu.VMEM((de, d), BF16),
                        pltpu.VMEM((MOE_X_DEPTH, MOE_BLOCK, d), F32), pltpu.VMEM((2, MOE_BLOCK, d), F32),
                        pltpu.SemaphoreType.DMA((MOE_X_DEPTH,)), pltpu.SemaphoreType.DMA((2,))],
    )
    return pl.pallas_call(
        functools.partial(_moe_body, n_blocks=cap // MOE_BLOCK),
        out_shape=jax.ShapeDtypeStruct((cap, d), F32),
        grid_spec=grid_spec,
        compiler_params=_params(("arbitrary",)),
        name="moe_experts",
    )(first_block, block_count, n_used, xb, w_gu, b_gu.reshape(ne, 1, d2), w_down, b_down.reshape(ne, 1, d))


def _combine_body(dest_ref, dest_next_ref, gt_ref, h1_ref, yb_hbm, g2_ref, b2_ref, o_ref, ybuf, sems, *, alpha):
    i = pl.program_id(0)
    slot = i % 2

    def row_copy(dref, s, t, kk):
        return pltpu.make_async_copy(yb_hbm.at[pl.ds(dref[0, 0, t * TOP_K + kk], 1)],
                                     ybuf.at[s, kk, pl.ds(t, 1)], sems.at[s])

    def issue(dref, s):
        def body(t, carry):
            for kk in range(TOP_K):
                row_copy(dref, s, t, kk).start(priority=kk % 2)
            return carry

        lax.fori_loop(0, COMBINE_TOKENS, body, 0, unroll=DMA_UNROLL)

    @pl.when(i == 0)
    def _prologue():
        issue(dest_ref, 0)

    @pl.when(i + 1 < pl.num_programs(0))
    def _prefetch():
        issue(dest_next_ref, 1 - slot)

    def drain(t, carry):
        for kk in range(TOP_K):
            row_copy(dest_ref, slot, t, kk).wait()
        return carry

    lax.fori_loop(0, COMBINE_TOKENS, drain, 0, unroll=DMA_UNROLL)

    gates = gt_ref[...]
    lane = lax.broadcasted_iota(jnp.int32, gates.shape, 1)
    ff = jnp.zeros(h1_ref.shape, F32)
    for kk in range(TOP_K):
        gk = jnp.sum(jnp.where(lane == kk, gates, 0.0), axis=1, keepdims=True)
        ff = ff + gk * ybuf[slot, kk]
    o_ref[0] = _layer_norm(alpha * h1_ref[...] + ff, g2_ref[...], b2_ref[...])


def _combine(h1, yb, dest, gates, g2, b2, bsz, seq, alpha):
    n, d = h1.shape
    steps = n // COMBINE_TOKENS
    per_b = seq // COMBINE_TOKENS
    dest3 = dest.reshape(steps, 1, COMBINE_TOKENS * TOP_K)
    dest_blk = (1, 1, COMBINE_TOKENS * TOP_K)
    return pl.pallas_call(
        functools.partial(_combine_body, alpha=alpha),
        out_shape=jax.ShapeDtypeStruct((bsz, seq, d), F32),
        grid=(steps,),
        in_specs=[
            pl.BlockSpec(dest_blk, lambda i: (i, 0, 0), memory_space=pltpu.SMEM),
            pl.BlockSpec(dest_blk, lambda i: (jnp.minimum(i + 1, steps - 1), 0, 0), memory_space=pltpu.SMEM),
            pl.BlockSpec((COMBINE_TOKENS, LANES), lambda i: (i, 0)),
            pl.BlockSpec((COMBINE_TOKENS, d), lambda i: (i, 0)),
            pl.BlockSpec(memory_space=pl.ANY),
            _const_spec((1, d)), _const_spec((1, d)),
        ],
        out_specs=pl.BlockSpec((1, COMBINE_TOKENS, d), lambda i: (i // per_b, i % per_b, 0)),
        scratch_shapes=[pltpu.VMEM((2, TOP_K, COMBINE_TOKENS, d), F32), pltpu.SemaphoreType.DMA((2,))],
        compiler_params=_params(("arbitrary",)),
        name="moe_combine_ln2",
    )(dest3, dest3, gates, h1, yb, g2, b2)


def _pad_cols(a, width):
    return jnp.pad(a, ((0, 0), (0, width - a.shape[1])))


def kernel(x, meta, ln0_g, ln0_b, w_in, b_fgate, fox_norm_g, rwkv_mu, w0, w2, a0, a2, g2, k_k, k_a, r_k, lnx_g, lnx_b, w_out, ln1_g, ln1_b, w_router, b_router, w_gu, b_gu, w_down, b_down, ln2_g, ln2_b):
    bsz, seq, d = x.shape
    depth = w_in.shape[0]
    assert depth == 1, "meta-token rows are dropped before the MoE, which is only valid for one layer"
    assert seq % TILE == 0 and seq % COMBINE_TOKENS == 0 and seq % DISPATCH_TOKENS == 0
    alpha = float((2 * depth) ** 0.25)
    n = bsz * seq
    ne = w_gu.shape[1]
    row = lambda a: a.reshape(1, -1)

    w = w_in[0]
    f3 = 3 * FOX_WIDTH
    r3 = 3 * RWKV_WIDTH
    wqkv = w[:, :f3].astype(BF16)
    wf = _pad_cols(w[:, f3:f3 + FOX_HEADS], LANES).astype(BF16)
    bf = _pad_cols(row(b_fgate[0]), LANES)
    wr_raw = w[:, f3 + FOX_HEADS:]

    def regroup(a):
        return jnp.concatenate([
            a[:, :r3],
            _pad_cols(a[:, r3:r3 + DECAY_LORA], LANES),
            _pad_cols(a[:, r3 + DECAY_LORA:r3 + DECAY_LORA + AAA_LORA], LANES),
            a[:, r3 + DECAY_LORA + AAA_LORA:],
        ], axis=1)

    wr = regroup(wr_raw).astype(BF16)
    mu = regroup(row(rwkv_mu[0]))
    w2p = jnp.pad(w2[0], ((0, LANES - DECAY_LORA), (0, 0))).astype(BF16)
    a2p = jnp.pad(a2[0], ((0, LANES - AAA_LORA), (0, 0))).astype(BF16)
    g2b = g2[0].astype(BF16)
    hid = jnp.arange(RWKV_WIDTH) // HEAD_DIM
    ones_bd = (hid[:, None] == hid[None, :]).astype(BF16)
    meta_tile = jnp.pad(meta.astype(x.dtype), ((TILE - N_META, 0), (0, 0)))

    qa, ka, va, zr = _inproj(x, meta_tile, row(ln0_g), row(ln0_b), wqkv, wf, bf, wr)
    o_fox = _fox_attention(qa, ka, va, row(fox_norm_g[0]))
    y_rwkv = _rwkv(zr, mu, row(w0[0]), w2p, row(a0[0]), a2p, g2b, row(k_k[0]), row(k_a[0]),
                   row(r_k[0]), row(lnx_g[0]), row(lnx_b[0]), ones_bd)

    wo = w_out[0].astype(BF16)
    w_rt = _pad_cols(w_router[0], LANES)
    w_rt_hi = w_rt.astype(BF16)
    w_rt_lo = (w_rt - w_rt_hi.astype(F32)).astype(BF16)
    b_rt = jnp.concatenate([row(b_router[0]), jnp.full((1, LANES - ne), NEG_BIG, F32)], axis=1)
    h1, e_pad, gates, rank_pad, counts_f = _outproj_router(
        x, o_fox, y_rwkv, row(ln0_g), row(ln0_b), wo[:FOX_WIDTH], wo[FOX_WIDTH:],
        row(ln1_g[0]), row(ln1_b[0]), w_rt_hi, w_rt_lo, b_rt, alpha)

    counts = counts_f[0, :ne].astype(jnp.int32)
    pcounts = (counts + MOE_BLOCK - 1) // MOE_BLOCK * MOE_BLOCK
    pend = jnp.cumsum(pcounts)
    pstart = pend - pcounts
    nblk = n * TOP_K // MOE_BLOCK + ne
    cap = nblk * MOE_BLOCK
    e_sel = e_pad[:, :TOP_K]
    onehot = (e_sel[:, :, None] == jnp.arange(ne, dtype=jnp.int32)[None, None, :]).astype(jnp.int32)
    dest = (jnp.sum(onehot * pstart[None, None, :], axis=-1) + rank_pad[:, :TOP_K]).reshape(-1)
    n_used = (pend[-1:] // MOE_BLOCK).astype(jnp.int32)
    zstart = jnp.maximum(pend - MOE_BLOCK, 0).astype(jnp.int32)
    first_block = (pstart // MOE_BLOCK).astype(jnp.int32)
    block_count = (pcounts // MOE_BLOCK).astype(jnp.int32)

    xb = _dispatch(h1, dest, zstart, n_used, cap)
    yb = _moe_experts(xb, first_block, block_count, n_used, w_gu[0], b_gu[0], w_down[0], b_down[0])
    return _combine(h1, yb, dest, gates, row(ln2_g[0]), row(ln2_b[0]), bsz, seq, alpha)
```

```python
import functools
import math

import jax
import jax.numpy as jnp
from jax import lax
from jax.experimental import pallas as pl
from jax.experimental.pallas import tpu as pltpu

N_META = 16
HEAD_DIM = 64
FOX_HEADS = 8
RWKV_HEADS = 8
FOX_WIDTH = FOX_HEADS * HEAD_DIM
RWKV_WIDTH = RWKV_HEADS * HEAD_DIM
DECAY_LORA = 64
AAA_LORA = 64
GATE_LORA = 128
TOP_K = 4
SWIGLU_LIMIT = 7.0
SWIGLU_ALPHA = 1.702
LN_EPS = 1e-5
GN_EPS = 64e-5
RMS_EPS = 1e-6
NEG_BIG = -1e30
LOG2E = math.log2(math.e)

LANES = 128
MXU_DIM = 256
TILE = 512
CHUNK = 64
FOX_ROW_BLOCK = 256
MOE_BLOCK = 256
MOE_X_DEPTH = 3
DISPATCH_TOKENS = 512
COMBINE_TOKENS = 512
DMA_UNROLL = 8
VMEM_LIMIT = 56 * 1024 * 1024

Q_BIAS_LANE = HEAD_DIM
K_BIAS_LANE = HEAD_DIM + 3
V_ONES_LANE = HEAD_DIM

F32 = jnp.float32
BF16 = jnp.bfloat16


def _dot(a, b):
    return jnp.dot(a, b, preferred_element_type=F32)


def _dot_nt(a, b):
    return lax.dot_general(a, b, (((1,), (1,)), ((), ())), preferred_element_type=F32)


def _dot_tn(a, b):
    return lax.dot_general(a, b, (((0,), (0,)), ((), ())), preferred_element_type=F32)


def _layer_norm(x, g, b):
    mu = jnp.mean(x, axis=-1, keepdims=True)
    xc = x - mu
    var = jnp.mean(xc * xc, axis=-1, keepdims=True)
    return xc * lax.rsqrt(var + LN_EPS) * g + b


def _softplus(x):
    return jnp.maximum(x, 0.0) + jnp.log1p(jnp.exp(-jnp.abs(x)))


def _row_cumsum(x, period, row):
    pos = row % period
    sh = 1
    while sh < period:
        x = x + jnp.where(pos >= sh, pltpu.roll(x, sh, 0), 0.0)
        sh *= 2
    return x


def _const_spec(shape):
    nd = len(shape)
    return pl.BlockSpec(shape, lambda *_: (0,) * nd)


def _params(sem):
    return pltpu.CompilerParams(dimension_semantics=sem, vmem_limit_bytes=VMEM_LIMIT)


def _inproj_body(x_ref, meta_ref, g_ref, b_ref, wqkv_ref, wf_ref, bf_ref, wr_ref,
                 qa_ref, ka_ref, va_ref, zr_ref, carry_s):
    i = pl.program_id(1)

    @pl.when(i == 0)
    def _reset():
        carry_s[...] = jnp.zeros(carry_s.shape, F32)

    x = jnp.where(i == 0, meta_ref[...], x_ref[0])
    h = _layer_norm(x, g_ref[...], b_ref[...])
    row = lax.broadcasted_iota(jnp.int32, (TILE, 1), 0)
    valid = jnp.logical_or(i > 0, row >= TILE - N_META)
    hb = jnp.where(valid, h, 0.0).astype(BF16)
    zr_ref[0] = _dot(hb, wr_ref[...]).astype(BF16)

    f = _dot(hb, wf_ref[...]) + bf_ref[...]
    lf = jnp.where(valid, -_softplus(-f), 0.0)
    cs = _row_cumsum(lf, TILE, row) + carry_s[...]
    carry_s[...] = cs[TILE - 1:TILE, :]
    c2 = cs * LOG2E

    qkv = _dot(hb, wqkv_ref[...])
    w = FOX_WIDTH
    qs = qkv[:, :w] * (LOG2E * HEAD_DIM ** -0.5)
    ks = qkv[:, w:2 * w]
    vs = qkv[:, 2 * w:]
    lane = lax.broadcasted_iota(jnp.int32, (TILE, LANES), 1)
    in_head = lane < HEAD_DIM
    for hd in range(FOX_HEADS):
        pair = slice(LANES * (hd // 2), LANES * (hd // 2 + 1))

        def head_lanes(a):
            slab = a[:, pair]
            return slab if hd % 2 == 0 else pltpu.roll(slab, HEAD_DIM, 1)

        col = jnp.broadcast_to(c2[:, hd:hd + 1], (TILE, LANES))
        hi = col.astype(BF16).astype(F32)
        rem = col - hi
        mid = rem.astype(BF16).astype(F32)
        lo = rem - mid
        q_bias = jnp.where(lane == Q_BIAS_LANE, hi, jnp.where(lane == Q_BIAS_LANE + 1, mid, lo))
        q_tail = jnp.where(lane < Q_BIAS_LANE + 3, q_bias, jnp.where(lane < Q_BIAS_LANE + 6, 1.0, 0.0))
        k_bias = jnp.where(lane == K_BIAS_LANE, -hi, jnp.where(lane == K_BIAS_LANE + 1, -mid, -lo))
        k_bias = jnp.where(valid, k_bias, jnp.where(lane == K_BIAS_LANE, NEG_BIG, 0.0))
        k_tail = jnp.where(lane < K_BIAS_LANE, 1.0, jnp.where(lane < K_BIAS_LANE + 3, k_bias, 0.0))
        qa_ref[0, hd] = jnp.where(in_head, head_lanes(qs), q_tail).astype(BF16)
        ka_ref[0, hd] = jnp.where(in_head, head_lanes(ks), k_tail).astype(BF16)
        va_ref[0, hd] = jnp.where(in_head, head_lanes(vs), jnp.where(lane == V_ONES_LANE, 1.0, 0.0)).astype(BF16)


def _inproj(x, meta_tile, g0, b0, wqkv, wf, bf, wr):
    bsz, seq, d = x.shape
    nt = seq // TILE + 1
    lp = nt * TILE
    rc = wr.shape[1]
    slab = jax.ShapeDtypeStruct((bsz, FOX_HEADS, lp, LANES), BF16)
    slab_spec = pl.BlockSpec((1, FOX_HEADS, TILE, LANES), lambda b, i: (b, 0, i, 0))
    return pl.pallas_call(
        _inproj_body,
        out_shape=(slab, slab, slab, jax.ShapeDtypeStruct((bsz, lp, rc), BF16)),
        grid=(bsz, nt),
        in_specs=[
            pl.BlockSpec((1, TILE, d), lambda b, i: (b, jnp.maximum(i - 1, 0), 0)),
            _const_spec((TILE, d)),
            _const_spec((1, d)), _const_spec((1, d)),
            _const_spec(wqkv.shape), _const_spec(wf.shape), _const_spec(bf.shape), _const_spec(wr.shape),
        ],
        out_specs=(slab_spec, slab_spec, slab_spec, pl.BlockSpec((1, TILE, rc), lambda b, i: (b, i, 0))),
        scratch_shapes=[pltpu.VMEM((1, LANES), F32)],
        compiler_params=_params(("parallel", "arbitrary")),
        name="ln0_inproj",
    )(x, meta_tile, g0, b0, wqkv, wf, bf, wr)


def _fox_body(q_ref, k_hbm, v_hbm, g_ref, o_ref, m_sc, acc_sc, kbuf, vbuf, ksem, vsem):
    b = pl.program_id(0)
    qi = pl.program_id(1)

    def kv_copy(ki, slot):
        rows = pl.ds(pl.multiple_of(ki * TILE, TILE), TILE)
        return (pltpu.make_async_copy(k_hbm.at[b, :, rows, :], kbuf.at[slot], ksem.at[slot]),
                pltpu.make_async_copy(v_hbm.at[b, :, rows, :], vbuf.at[slot], vsem.at[slot]))

    def start(ki, slot):
        for cp in kv_copy(ki, slot):
            cp.start()

    def wait(ki, slot):
        for cp in kv_copy(ki, slot):
            cp.wait()

    start(0, 0)
    m_sc[...] = jnp.full(m_sc.shape, NEG_BIG, F32)
    acc_sc[...] = jnp.zeros(acc_sc.shape, F32)

    heads = range(FOX_HEADS)

    def scores(hh, slot):
        return _dot_nt(q_ref[0, hh], kbuf[slot, hh])

    def softmax_pv(hh, s, causal, slot):
        for rb in range(TILE // FOX_ROW_BLOCK):
            rows = slice(rb * FOX_ROW_BLOCK, (rb + 1) * FOX_ROW_BLOCK)
            sb = s[rows]
            if causal:
                rr = lax.broadcasted_iota(jnp.int32, (FOX_ROW_BLOCK, TILE), 0) + rb * FOX_ROW_BLOCK
                cc = lax.broadcasted_iota(jnp.int32, (FOX_ROW_BLOCK, TILE), 1)
                sb = jnp.where(cc <= rr, sb, NEG_BIG)
            m_old = m_sc[hh, rows]
            m_col = jnp.maximum(m_old[:, 0:1], jnp.max(sb, axis=1, keepdims=True))
            m_new = jnp.broadcast_to(m_col, (FOX_ROW_BLOCK, LANES))
            p = jnp.exp2(sb - m_col).astype(BF16)
            acc_sc[hh, rows] = jnp.exp2(m_old - m_new) * acc_sc[hh, rows] + _dot(p, vbuf[slot, hh])
            m_sc[hh, rows] = m_new

    def step(causal, slot):
        s_next = scores(0, slot)
        for hh in heads:
            s_cur = s_next
            if hh + 1 < FOX_HEADS:
                s_next = scores(hh + 1, slot)
            softmax_pv(hh, s_cur, causal, slot)

    def below_diagonal(ki, carry):
        slot = ki % 2
        start(ki + 1, 1 - slot)
        wait(ki, slot)
        step(False, slot)
        return carry

    lax.fori_loop(0, qi, below_diagonal, 0)

    slot = qi % 2
    wait(qi, slot)
    step(True, slot)
    lane = lax.broadcasted_iota(jnp.int32, (TILE, LANES), 1)
    lo = lane < HEAD_DIM
    normed = []
    for hh in heads:
        acc = acc_sc[hh]
        o = acc / acc[:, V_ONES_LANE:V_ONES_LANE + 1]
        ms = jnp.sum(jnp.where(lo, o * o, 0.0), axis=1, keepdims=True) * (1.0 / HEAD_DIM)
        normed.append(o * lax.rsqrt(ms + RMS_EPS))
    for pr in range(FOX_HEADS // 2):
        o = jnp.where(lo, normed[2 * pr], pltpu.roll(normed[2 * pr + 1], HEAD_DIM, 1))
        sl = slice(pr * LANES, (pr + 1) * LANES)
        o_ref[0, :, sl] = (o * g_ref[:, sl]).astype(BF16)


def _fox_attention(qa, ka, va, g):
    bsz, _, lp, _ = qa.shape
    nt = lp // TILE
    slab = (FOX_HEADS, TILE, LANES)
    return pl.pallas_call(
        _fox_body,
        out_shape=jax.ShapeDtypeStruct((bsz, lp, FOX_WIDTH), BF16),
        grid=(bsz, nt),
        in_specs=[
            pl.BlockSpec((1,) + slab, lambda b, qi: (b, 0, qi, 0)),
            pl.BlockSpec(memory_space=pl.ANY),
            pl.BlockSpec(memory_space=pl.ANY),
            _const_spec((1, FOX_WIDTH)),
        ],
        out_specs=pl.BlockSpec((1, TILE, FOX_WIDTH), lambda b, qi: (b, qi, 0)),
        scratch_shapes=[pltpu.VMEM(slab, F32), pltpu.VMEM(slab, F32),
                        pltpu.VMEM((2,) + slab, BF16), pltpu.VMEM((2,) + slab, BF16),
                        pltpu.SemaphoreType.DMA((2,)), pltpu.SemaphoreType.DMA((2,))],
        compiler_params=_params(("parallel", "parallel")),
        name="fox_attention",
    )(qa, ka, va, g)


def _interleave(*gens):
    gens = list(gens)
    while gens:
        for gen in list(gens):
            try:
                next(gen)
            except StopIteration:
                gens.remove(gen)


def _rwkv_body(z_ref, zp_ref, mu_ref, w0_ref, w2_ref, a0_ref, a2_ref, g2_ref, kk_ref, ka_ref, rk_ref,
               lg_ref, lb_ref, ones_ref, y_ref,
               ar_s, bt_s, kt_s, v_s, bkb_s, gam_s, s_s, y_s, bon_s, gate_s, low_s, arb_s, pv_s):
    i = pl.program_id(1)
    nch = TILE // CHUNK
    w = RWKV_WIDTH
    gw = MXU_DIM
    ngrp = w // gw

    @pl.when(i == 0)
    def _reset():
        s_s[...] = jnp.zeros(s_s.shape, F32)

    z = z_ref[0].astype(F32)
    prev = zp_ref[0][15:16, :].astype(F32)
    prev = jnp.where(i == 0, 0.0, prev)
    row = lax.broadcasted_iota(jnp.int32, (TILE, 1), 0)
    z_shift = jnp.where(row == 0, prev, pltpu.roll(z, 1, 0))
    zs = z + (z_shift - z) * mu_ref[...]
    r = zs[:, 0:w]
    kr = zs[:, w:2 * w]
    vr = zs[:, 2 * w:3 * w]
    dw = zs[:, 3 * w:3 * w + LANES]
    da = zs[:, 3 * w + LANES:3 * w + 2 * LANES]
    dg = zs[:, 3 * w + 2 * LANES:3 * w + 3 * LANES]
    lw = -math.exp(-0.5) * jax.nn.sigmoid(w0_ref[...] + _dot(jnp.tanh(dw).astype(BF16), w2_ref[...]))
    alpha = jax.nn.sigmoid(a0_ref[...] + _dot(da.astype(BF16), a2_ref[...]))
    gate_s[...] = _dot(jax.nn.sigmoid(dg).astype(BF16), g2_ref[...])
    ones = ones_ref[...]
    kk0 = kr * kk_ref[...]
    kk = kk0 * lax.rsqrt(jnp.maximum(_dot((kk0 * kk0).astype(BF16), ones), 1e-24))
    kmod = kr * (1.0 + (alpha - 1.0) * ka_ref[...])
    bon_s[...] = _dot((r * kmod * rk_ref[...]).astype(BF16), ones) * vr

    gcum = _row_cumsum(lw, CHUNK, row)
    glast = jnp.concatenate(
        [jnp.broadcast_to(gcum[c * CHUNK + CHUNK - 1:(c + 1) * CHUNK, :], (CHUNK, w)) for c in range(nch)], axis=0)
    e_out = jnp.exp(-gcum)
    e_tail = jnp.exp(glast - gcum)
    a_t = ((-kk) * jnp.exp(gcum - lw)).astype(BF16)
    r_t = (r * jnp.exp(gcum)).astype(BF16)
    bt_s[...] = (kk * alpha * e_out).astype(BF16)
    kt_s[...] = (kmod * e_out).astype(BF16)
    v_s[...] = vr.astype(BF16)
    b_bar = (kk * alpha * e_tail).astype(BF16)
    k_bar = (kmod * e_tail).astype(BF16)
    gam = jnp.exp(glast)
    for c in range(nch):
        rows = slice(c * CHUNK, (c + 1) * CHUNK)
        ar_s[c, 0:CHUNK, :] = a_t[rows]
        ar_s[c, CHUNK:2 * CHUNK, :] = r_t[rows]
        bkb_s[c, 0:CHUNK, :] = b_bar[rows]
        bkb_s[c, CHUNK:2 * CHUNK, :] = k_bar[rows]
        gam_s[c] = gam[c * CHUNK:c * CHUNK + 1]

    rb = lax.broadcasted_iota(jnp.int32, (gw, gw), 0)
    cb = lax.broadcasted_iota(jnp.int32, (gw, gw), 1)
    same_head = (rb // HEAD_DIM) == (cb // HEAD_DIM)
    r2 = lax.broadcasted_iota(jnp.int32, (2 * CHUNK, gw), 0)
    c2 = lax.broadcasted_iota(jnp.int32, (2 * CHUNK, gw), 1) % CHUNK
    tri2 = c2 < jnp.where(r2 < CHUNK, r2, r2 - CHUNK + 1)
    r1 = lax.broadcasted_iota(jnp.int32, (CHUNK, gw), 0)
    c1 = lax.broadcasted_iota(jnp.int32, (CHUNK, gw), 1) % CHUNK
    level_masks = []
    s = 1
    while s < CHUNK:
        level_masks.append(jnp.logical_and(((r1 ^ c1) & (-s)) == s, r1 > c1))
        s *= 2

    def block_diag(x):
        xb = x.astype(BF16)
        return jnp.where(same_head, jnp.concatenate([xb] * (gw // CHUNK), axis=0), jnp.zeros((), BF16))

    def lanes(g):
        return slice(g * gw, (g + 1) * gw)

    def intra_chunk(chunks):
        keys = [(c, g) for c in chunks for g in range(ngrp)]
        n_ab, low = {}, {}
        for c, g in keys:
            rows = slice(c * CHUNK, (c + 1) * CHUNK)
            ar = ar_s[c, :, lanes(g)]
            pb = jnp.where(tri2, _dot_nt(ar, block_diag(bt_s[rows, lanes(g)])), 0.0)
            pk = jnp.where(tri2, _dot_nt(ar, block_diag(kt_s[rows, lanes(g)])), 0.0)
            n_ab[c, g] = pb[0:CHUNK]
            arb_s[c, :, lanes(g)] = pb[CHUNK:2 * CHUNK].astype(BF16)
            pv_s[c, :, lanes(g)] = _dot(pk.astype(BF16), block_diag(v_s[rows, lanes(g)]))
            low[c, g] = jnp.where(level_masks[0], n_ab[c, g], 0.0)
        yield
        for m in level_masks[1:]:
            x = {}
            for key in keys:
                n_off = jnp.where(m, n_ab[key], 0.0)
                x[key] = n_off + _dot(n_off.astype(BF16), block_diag(low[key]))
            yield
            for key in keys:
                low[key] = low[key] + x[key] + _dot(low[key].astype(BF16), block_diag(x[key]))
            yield
        for c, g in keys:
            low_s[c, :, lanes(g)] = low[c, g].astype(BF16)

    def inter_chunk(chunks):
        for c in chunks:
            rows = slice(c * CHUNK, (c + 1) * CHUNK)
            st, ars, u = {}, {}, {}
            for g in range(ngrp):
                st[g] = s_s[g]
                ars[g] = _dot_nt(ar_s[c, :, lanes(g)], st[g].astype(BF16))
            yield
            for g in range(ngrp):
                wm = ars[g][0:CHUNK] + pv_s[c, 0:CHUNK, lanes(g)]
                u[g] = wm + _dot(low_s[c, :, lanes(g)], block_diag(wm))
            yield
            for g in range(ngrp):
                y_s[rows, lanes(g)] = (ars[g][CHUNK:2 * CHUNK] + pv_s[c, CHUNK:2 * CHUNK, lanes(g)]
                                       + _dot(arb_s[c, :, lanes(g)], block_diag(u[g])))
                uv = jnp.concatenate([u[g].astype(BF16), v_s[rows, lanes(g)]], axis=0)
                upd = _dot_tn(uv, bkb_s[c, :, lanes(g)])
                s_s[g] = st[g] * gam_s[c][:, lanes(g)] + jnp.where(same_head, upd, 0.0)
            yield

    half = nch // 2
    _interleave(intra_chunk(range(0, half)))
    _interleave(intra_chunk(range(half, nch)), inter_chunk(range(0, half)))
    _interleave(inter_chunk(range(half, nch)))

    y = y_s[...]
    mean = _dot(y.astype(BF16), ones) * (1.0 / HEAD_DIM)
    d = y - mean
    var = _dot((d * d).astype(BF16), ones) * (1.0 / HEAD_DIM)
    yn = d * lax.rsqrt(var + GN_EPS)
    y_ref[0] = ((yn * lg_ref[...] + lb_ref[...] + bon_s[...]) * gate_s[...]).astype(BF16)


def _rwkv(zr, mu, w0, w2, a0, a2, g2, k_k, k_a, r_k, lnx_g, lnx_b, ones_bd):
    bsz, lp, rc = zr.shape
    nt = lp // TILE
    nch = TILE // CHUNK
    w = RWKV_WIDTH
    vec = _const_spec((1, w))
    return pl.pallas_call(
        _rwkv_body,
        out_shape=jax.ShapeDtypeStruct((bsz, lp, w), BF16),
        grid=(bsz, nt),
        in_specs=[
            pl.BlockSpec((1, TILE, rc), lambda b, i: (b, i, 0)),
            pl.BlockSpec((1, 16, rc), lambda b, i: (b, jnp.maximum(i * (TILE // 16) - 1, 0), 0)),
            _const_spec((1, rc)),
            vec, _const_spec(w2.shape), vec, _const_spec(a2.shape), _const_spec(g2.shape),
            vec, vec, vec, vec, vec,
            _const_spec(ones_bd.shape),
        ],
        out_specs=pl.BlockSpec((1, TILE, w), lambda b, i: (b, i, 0)),
        scratch_shapes=[
            pltpu.VMEM((nch, 2 * CHUNK, w), BF16),
            pltpu.VMEM((TILE, w), BF16),
            pltpu.VMEM((TILE, w), BF16),
            pltpu.VMEM((TILE, w), BF16),
            pltpu.VMEM((nch, 2 * CHUNK, w), BF16),
            pltpu.VMEM((nch, 1, w), F32),
            pltpu.VMEM((w // MXU_DIM, MXU_DIM, MXU_DIM), F32),
            pltpu.VMEM((TILE, w), F32),
            pltpu.VMEM((TILE, w), F32),
            pltpu.VMEM((TILE, w), F32),
            pltpu.VMEM((nch, CHUNK, w), BF16),
            pltpu.VMEM((nch, CHUNK, w), BF16),
            pltpu.VMEM((nch, 2 * CHUNK, w), F32),
        ],
        compiler_params=_params(("parallel", "arbitrary")),
        name="rwkv7_mix",
    )(zr, zr, mu, w0, w2, a0, a2, g2, k_k, k_a, r_k, lnx_g, lnx_b, ones_bd)


def _outproj_body(x_ref, of_ref, yr_ref, g0_ref, b0_ref, wo1_ref, wo2_ref, g1_ref, b1_ref, wrh_ref, wrl_ref, br_ref,
                  h1_ref, e_ref, gt_ref, rk_ref, cnt_ref, carry_s, *, alpha):
    first = jnp.logical_and(pl.program_id(0) == 0, pl.program_id(1) == 0)

    @pl.when(first)
    def _reset():
        carry_s[...] = jnp.zeros(carry_s.shape, F32)

    h0 = _layer_norm(x_ref[0], g0_ref[...], b0_ref[...])
    mix = _dot(of_ref[0], wo1_ref[...]) + _dot(yr_ref[0], wo2_ref[...])
    h1 = _layer_norm(alpha * h0 + mix, g1_ref[...], b1_ref[...])
    h1_ref[...] = h1
    h_hi = h1.astype(BF16)
    h_lo = (h1 - h_hi.astype(F32)).astype(BF16)
    logits = ((_dot(h_hi, wrh_ref[...]) + _dot(h_lo, wrl_ref[...]))
              + (_dot(h_hi, wrl_ref[...]) + _dot(h_lo, wrh_ref[...]))) + br_ref[...]
    lane = lax.broadcasted_iota(jnp.int32, (TILE, LANES), 1)
    lane_f = lane.astype(F32)
    cur = logits
    vals, idxs = [], []
    for _ in range(TOP_K):
        m = jnp.max(cur, axis=1, keepdims=True)
        idx = jnp.min(jnp.where(cur == m, lane_f, float(LANES)), axis=1, keepdims=True).astype(jnp.int32)
        vals.append(m)
        idxs.append(idx)
        cur = jnp.where(lane == idx, -jnp.inf, cur)
    exps = [jnp.exp(v - vals[0]) for v in vals]
    denom = exps[0] + exps[1] + exps[2] + exps[3]
    hot = [lane == idx for idx in idxs]
    multi = jnp.zeros((TILE, LANES), F32)
    for hk in hot:
        multi = multi + hk.astype(F32)
    rr = lax.broadcasted_iota(jnp.int32, (TILE, TILE), 0)
    cc = lax.broadcasted_iota(jnp.int32, (TILE, TILE), 1)
    before = _dot((cc < rr).astype(BF16), multi.astype(BF16)) + carry_s[...]
    e_out = jnp.zeros((TILE, LANES), jnp.int32)
    g_out = jnp.zeros((TILE, LANES), F32)
    r_out = jnp.zeros((TILE, LANES), F32)
    for kk in range(TOP_K):
        rank = jnp.sum(jnp.where(hot[kk], before, 0.0), axis=1, keepdims=True)
        e_out = jnp.where(lane == kk, idxs[kk], e_out)
        g_out = jnp.where(lane == kk, exps[kk] / denom, g_out)
        r_out = jnp.where(lane == kk, rank, r_out)
    e_ref[...] = e_out
    gt_ref[...] = g_out
    rk_ref[...] = r_out.astype(jnp.int32)
    carry_s[...] = carry_s[...] + jnp.sum(multi, axis=0, keepdims=True)
    cnt_ref[...] = carry_s[...]


def _outproj_router(x, o_fox, y_rwkv, g0, b0, wo1, wo2, g1, b1, w_router_hi, w_router_lo, b_router, alpha):
    bsz, seq, d = x.shape
    nt = seq // TILE
    n = bsz * seq
    row_blk = lambda c: pl.BlockSpec((TILE, c), lambda b, i: (b * nt + i, 0))
    mix_spec = pl.BlockSpec((1, TILE, FOX_WIDTH), lambda b, i: (b, i + 1, 0))
    return pl.pallas_call(
        functools.partial(_outproj_body, alpha=alpha),
        out_shape=(jax.ShapeDtypeStruct((n, d), F32),
                   jax.ShapeDtypeStruct((n, LANES), jnp.int32),
                   jax.ShapeDtypeStruct((n, LANES), F32),
                   jax.ShapeDtypeStruct((n, LANES), jnp.int32),
                   jax.ShapeDtypeStruct((1, LANES), F32)),
        grid=(bsz, nt),
        in_specs=[
            pl.BlockSpec((1, TILE, d), lambda b, i: (b, i, 0)),
            mix_spec, mix_spec,
            _const_spec((1, d)), _const_spec((1, d)),
            _const_spec(wo1.shape), _const_spec(wo2.shape),
            _const_spec((1, d)), _const_spec((1, d)),
            _const_spec(w_router_hi.shape), _const_spec(w_router_lo.shape), _const_spec(b_router.shape),
        ],
        out_specs=(row_blk(d), row_blk(LANES), row_blk(LANES), row_blk(LANES), _const_spec((1, LANES))),
        scratch_shapes=[pltpu.VMEM((1, LANES), F32)],
        compiler_params=_params(("arbitrary", "arbitrary")),
        name="outproj_ln1_router",
    )(x, o_fox, y_rwkv, g0, b0, wo1, wo2, g1, b1, w_router_hi, w_router_lo, b_router)


def _dispatch_body(dest_ref, zstart_ref, nused_ref, h1_ref, xb_hbm, zero_buf, sem, zsem, *, n_experts, n_blocks):
    i = pl.program_id(0)

    def zero_copy(start):
        start = pl.multiple_of(start, MOE_BLOCK)
        return pltpu.make_async_copy(zero_buf, xb_hbm.at[pl.ds(start, MOE_BLOCK)], zsem)

    @pl.when(i == 0)
    def _clear():
        zero_buf[...] = jnp.zeros(zero_buf.shape, F32)
        for e in range(n_experts):
            zero_copy(zstart_ref[e]).start()

        def start_tail(j, carry):
            zero_copy(j * MOE_BLOCK).start()
            return carry

        lax.fori_loop(nused_ref[0], n_blocks, start_tail, 0)
        for e in range(n_experts):
            zero_copy(zstart_ref[e]).wait()

        def wait_tail(j, carry):
            zero_copy(j * MOE_BLOCK).wait()
            return carry

        lax.fori_loop(nused_ref[0], n_blocks, wait_tail, 0)

    def row_copy(t, kk):
        return pltpu.make_async_copy(h1_ref.at[pl.ds(t, 1)],
                                     xb_hbm.at[pl.ds(dest_ref[0, 0, t * TOP_K + kk], 1)], sem)

    def issue(t, carry):
        for kk in range(TOP_K):
            row_copy(t, kk).start(priority=kk % 2)
        return carry

    lax.fori_loop(0, DISPATCH_TOKENS, issue, 0, unroll=DMA_UNROLL)

    def drain(t, carry):
        for kk in range(TOP_K):
            row_copy(t, kk).wait()
        return carry

    lax.fori_loop(0, DISPATCH_TOKENS, drain, 0, unroll=DMA_UNROLL)


def _dispatch(h1, dest, zstart, n_used, cap):
    n, d = h1.shape
    steps = n // DISPATCH_TOKENS
    dest3 = dest.reshape(steps, 1, DISPATCH_TOKENS * TOP_K)
    return pl.pallas_call(
        functools.partial(_dispatch_body, n_experts=zstart.shape[0], n_blocks=cap // MOE_BLOCK),
        out_shape=jax.ShapeDtypeStruct((cap, d), F32),
        grid=(steps,),
        in_specs=[
            pl.BlockSpec((1, 1, DISPATCH_TOKENS * TOP_K), lambda i: (i, 0, 0), memory_space=pltpu.SMEM),
            pl.BlockSpec(memory_space=pltpu.SMEM),
            pl.BlockSpec(memory_space=pltpu.SMEM),
            pl.BlockSpec((DISPATCH_TOKENS, d), lambda i: (i, 0)),
        ],
        out_specs=pl.BlockSpec(memory_space=pl.ANY),
        scratch_shapes=[pltpu.VMEM((MOE_BLOCK, d), F32), pltpu.SemaphoreType.DMA(()), pltpu.SemaphoreType.DMA(())],
        compiler_params=_params(("arbitrary",)),
        name="moe_dispatch",
    )(dest3, zstart, n_used, h1)


def _moe_body(first_ref, count_ref, nused_ref, xb_hbm, wgu_ref, bgu_ref, wd_ref, bd_ref, yb_hbm,
              wgu_bf, wd_bf, xbuf, ybuf, xsem, ysem, *, n_blocks):
    e = pl.program_id(0)
    first = first_ref[e]
    nb = count_ref[e]
    nused = nused_ref[0]
    x_depth = xbuf.shape[0]
    y_depth = ybuf.shape[0]

    def block_rows(g):
        return pl.ds(pl.multiple_of(g * MOE_BLOCK, MOE_BLOCK), MOE_BLOCK)

    def x_copy(g):
        s = g % x_depth
        return pltpu.make_async_copy(xb_hbm.at[block_rows(g)], xbuf.at[s], xsem.at[s])

    def y_copy(g):
        s = g % y_depth
        return pltpu.make_async_copy(ybuf.at[s], yb_hbm.at[block_rows(g)], ysem.at[s])

    @pl.when(e == 0)
    def _prime():
        for g0 in range(x_depth - 1):
            @pl.when(g0 < nused)
            def _():
                x_copy(g0).start()

    wgu_bf[...] = wgu_ref[0].astype(BF16)
    wd_bf[...] = wd_ref[0].astype(BF16)
    de = wd_bf.shape[0]

    def body(b, carry):
        g = first + b

        @pl.when(g + x_depth - 1 < nused)
        def _prefetch():
            x_copy(g + x_depth - 1).start()

        x_copy(g).wait()

        @pl.when(g >= y_depth)
        def _reuse():
            y_copy(g - y_depth).wait()

        gu = _dot(xbuf[g % x_depth].astype(BF16), wgu_bf[...]) + bgu_ref[0]
        gate = jnp.minimum(gu[:, :de], SWIGLU_LIMIT)
        up = jnp.clip(gu[:, de:], -SWIGLU_LIMIT, SWIGLU_LIMIT)
        act = (up + 1.0) * (gate * jax.nn.sigmoid(gate * SWIGLU_ALPHA))
        ybuf[g % y_depth] = _dot(act.astype(BF16), wd_bf[...]) + bd_ref[0]
        y_copy(g).start()
        return carry

    lax.fori_loop(0, nb, body, 0)

    @pl.when(e == pl.num_programs(0) - 1)
    def _clear_tail():
        for back in range(y_depth, 0, -1):
            @pl.when(nused >= back)
            def _():
                y_copy(nused - back).wait()

        ybuf[0] = jnp.zeros(ybuf.shape[1:], F32)

        def tail_copy(j):
            rows = pl.ds(pl.multiple_of(j * MOE_BLOCK, MOE_BLOCK), MOE_BLOCK)
            return pltpu.make_async_copy(ybuf.at[0], yb_hbm.at[rows], ysem.at[0])

        def start_tail(j, carry):
            tail_copy(j).start()
            return carry

        def wait_tail(j, carry):
            tail_copy(j).wait()
            return carry

        lax.fori_loop(nused_ref[0], n_blocks, start_tail, 0)
        lax.fori_loop(nused_ref[0], n_blocks, wait_tail, 0)


def _moe_experts(xb, first_block, block_count, n_used, w_gu, b_gu, w_down, b_down):
    cap, d = xb.shape
    ne, _, d2 = w_gu.shape
    de = w_down.shape[1]
    wsel = lambda e, *_: (e, 0, 0)
    grid_spec = pltpu.PrefetchScalarGridSpec(
        num_scalar_prefetch=3,
        grid=(ne,),
        in_specs=[
            pl.BlockSpec(memory_space=pl.ANY),
            pl.BlockSpec((1, d, d2), wsel),
            pl.BlockSpec((1, 1, d2), wsel),
            pl.BlockSpec((1, de, d), wsel),
            pl.BlockSpec((1, 1, d), wsel),
        ],
        out_specs=pl.BlockSpec(memory_space=pl.ANY),
        scratch_shapes=[pltpu.VMEM((d, d2), BF16), pltpu.VMEM((de, d), BF16),
                        pltpu.VMEM((MOE_X_DEPTH, MOE_BLOCK, d), F32), pltpu.VMEM((2, MOE_BLOCK, d), F32),
                        pltpu.SemaphoreType.DMA((MOE_X_DEPTH,)), pltpu.SemaphoreType.DMA((2,))],
    )
    return pl.pallas_call(
        functools.partial(_moe_body, n_blocks=cap // MOE_BLOCK),
        out_shape=jax.ShapeDtypeStruct((cap, d), F32),
        grid_spec=grid_spec,
        compiler_params=_params(("arbitrary",)),
        name="moe_experts",
    )(first_block, block_count, n_used, xb, w_gu, b_gu.reshape(ne, 1, d2), w_down, b_down.reshape(ne, 1, d))


def _combine_body(dest_ref, dest_next_ref, gt_ref, h1_ref, yb_hbm, g2_ref, b2_ref, o_ref, ybuf, sems, *, alpha):
    i = pl.program_id(0)
    slot = i % 2

    def row_copy(dref, s, t, kk):
        return pltpu.make_async_copy(yb_hbm.at[pl.ds(dref[0, 0, t * TOP_K + kk], 1)],
                                     ybuf.at[s, kk, pl.ds(t, 1)], sems.at[s])

    def issue(dref, s):
        def body(t, carry):
            for kk in range(TOP_K):
                row_copy(dref, s, t, kk).start(priority=kk % 2)
            return carry

        lax.fori_loop(0, COMBINE_TOKENS, body, 0, unroll=DMA_UNROLL)

    @pl.when(i == 0)
    def _prologue():
        issue(dest_ref, 0)

    @pl.when(i + 1 < pl.num_programs(0))
    def _prefetch():
        issue(dest_next_ref, 1 - slot)

    def drain(t, carry):
        for kk in range(TOP_K):
            row_copy(dest_ref, slot, t, kk).wait()
        return carry

    lax.fori_loop(0, COMBINE_TOKENS, drain, 0, unroll=DMA_UNROLL)

    gates = gt_ref[...]
    lane = lax.broadcasted_iota(jnp.int32, gates.shape, 1)
    ff = jnp.zeros(h1_ref.shape, F32)
    for kk in range(TOP_K):
        gk = jnp.sum(jnp.where(lane == kk, gates, 0.0), axis=1, keepdims=True)
        ff = ff + gk * ybuf[slot, kk]
    o_ref[0] = _layer_norm(alpha * h1_ref[...] + ff, g2_ref[...], b2_ref[...])


def _combine(h1, yb, dest, gates, g2, b2, bsz, seq, alpha):
    n, d = h1.shape
    steps = n // COMBINE_TOKENS
    per_b = seq // COMBINE_TOKENS
    dest3 = dest.reshape(steps, 1, COMBINE_TOKENS * TOP_K)
    dest_blk = (1, 1, COMBINE_TOKENS * TOP_K)
    return pl.pallas_call(
        functools.partial(_combine_body, alpha=alpha),
        out_shape=jax.ShapeDtypeStruct((bsz, seq, d), F32),
        grid=(steps,),
        in_specs=[
            pl.BlockSpec(dest_blk, lambda i: (i, 0, 0), memory_space=pltpu.SMEM),
            pl.BlockSpec(dest_blk, lambda i: (jnp.minimum(i + 1, steps - 1), 0, 0), memory_space=pltpu.SMEM),
            pl.BlockSpec((COMBINE_TOKENS, LANES), lambda i: (i, 0)),
            pl.BlockSpec((COMBINE_TOKENS, d), lambda i: (i, 0)),
            pl.BlockSpec(memory_space=pl.ANY),
            _const_spec((1, d)), _const_spec((1, d)),
        ],
        out_specs=pl.BlockSpec((1, COMBINE_TOKENS, d), lambda i: (i // per_b, i % per_b, 0)),
        scratch_shapes=[pltpu.VMEM((2, TOP_K, COMBINE_TOKENS, d), F32), pltpu.SemaphoreType.DMA((2,))],
        compiler_params=_params(("arbitrary",)),
        name="moe_combine_ln2",
    )(dest3, dest3, gates, h1, yb, g2, b2)


def _pad_cols(a, width):
    return jnp.pad(a, ((0, 0), (0, width - a.shape[1])))


def kernel(x, meta, ln0_g, ln0_b, w_in, b_fgate, fox_norm_g, rwkv_mu, w0, w2, a0, a2, g2, k_k, k_a, r_k, lnx_g, lnx_b, w_out, ln1_g, ln1_b, w_router, b_router, w_gu, b_gu, w_down, b_down, ln2_g, ln2_b):
    bsz, seq, d = x.shape
    depth = w_in.shape[0]
    assert depth == 1, "meta-token rows are dropped before the MoE, which is only valid for one layer"
    assert seq % TILE == 0 and seq % COMBINE_TOKENS == 0 and seq % DISPATCH_TOKENS == 0
    alpha = float((2 * depth) ** 0.25)
    n = bsz * seq
    ne = w_gu.shape[1]
    row = lambda a: a.reshape(1, -1)

    w = w_in[0]
    f3 = 3 * FOX_WIDTH
    r3 = 3 * RWKV_WIDTH
    wqkv = w[:, :f3].astype(BF16)
    wf = _pad_cols(w[:, f3:f3 + FOX_HEADS], LANES).astype(BF16)
    bf = _pad_cols(row(b_fgate[0]), LANES)
    wr_raw = w[:, f3 + FOX_HEADS:]

    def regroup(a):
        return jnp.concatenate([
            a[:, :r3],
            _pad_cols(a[:, r3:r3 + DECAY_LORA], LANES),
            _pad_cols(a[:, r3 + DECAY_LORA:r3 + DECAY_LORA + AAA_LORA], LANES),
            a[:, r3 + DECAY_LORA + AAA_LORA:],
        ], axis=1)

    wr = regroup(wr_raw).astype(BF16)
    mu = regroup(row(rwkv_mu[0]))
    w2p = jnp.pad(w2[0], ((0, LANES - DECAY_LORA), (0, 0))).astype(BF16)
    a2p = jnp.pad(a2[0], ((0, LANES - AAA_LORA), (0, 0))).astype(BF16)
    g2b = g2[0].astype(BF16)
    hid = jnp.arange(RWKV_WIDTH) // HEAD_DIM
    ones_bd = (hid[:, None] == hid[None, :]).astype(BF16)
    meta_tile = jnp.pad(meta.astype(x.dtype), ((TILE - N_META, 0), (0, 0)))

    qa, ka, va, zr = _inproj(x, meta_tile, row(ln0_g), row(ln0_b), wqkv, wf, bf, wr)
    o_fox = _fox_attention(qa, ka, va, row(fox_norm_g[0]))
    y_rwkv = _rwkv(zr, mu, row(w0[0]), w2p, row(a0[0]), a2p, g2b, row(k_k[0]), row(k_a[0]),
                   row(r_k[0]), row(lnx_g[0]), row(lnx_b[0]), ones_bd)

    wo = w_out[0].astype(BF16)
    w_rt = _pad_cols(w_router[0], LANES)
    w_rt_hi = w_rt.astype(BF16)
    w_rt_lo = (w_rt - w_rt_hi.astype(F32)).astype(BF16)
    b_rt = jnp.concatenate([row(b_router[0]), jnp.full((1, LANES - ne), NEG_BIG, F32)], axis=1)
    h1, e_pad, gates, rank_pad, counts_f = _outproj_router(
        x, o_fox, y_rwkv, row(ln0_g), row(ln0_b), wo[:FOX_WIDTH], wo[FOX_WIDTH:],
        row(ln1_g[0]), row(ln1_b[0]), w_rt_hi, w_rt_lo, b_rt, alpha)

    counts = counts_f[0, :ne].astype(jnp.int32)
    pcounts = (counts + MOE_BLOCK - 1) // MOE_BLOCK * MOE_BLOCK
    pend = jnp.cumsum(pcounts)
    pstart = pend - pcounts
    nblk = n * TOP_K // MOE_BLOCK + ne
    cap = nblk * MOE_BLOCK
    e_sel = e_pad[:, :TOP_K]
    onehot = (e_sel[:, :, None] == jnp.arange(ne, dtype=jnp.int32)[None, None, :]).astype(jnp.int32)
    dest = (jnp.sum(onehot * pstart[None, None, :], axis=-1) + rank_pad[:, :TOP_K]).reshape(-1)
    n_used = (pend[-1:] // MOE_BLOCK).astype(jnp.int32)
    zstart = jnp.maximum(pend - MOE_BLOCK, 0).astype(jnp.int32)
    first_block = (pstart // MOE_BLOCK).astype(jnp.int32)
    block_count = (pcounts // MOE_BLOCK).astype(jnp.int32)

    xb = _dispatch(h1, dest, zstart, n_used, cap)
    yb = _moe_experts(xb, first_block, block_count, n_used, w_gu[0], b_gu[0], w_down[0], b_down[0])
    return _combine(h1, yb, dest, gates, row(ln2_g[0]), row(ln2_b[0]), bsz, seq, alpha)
```
